```python
import jax, jax.numpy as jnp
from jax import lax
import numpy as np

D_MODEL = 2048
BATCH = 2
SEQ = 16384
DEPTH = 1
DEC_BATCH = 8
DEC_SEQ = 4096
PAST_LEN = 128

A_HEADS = 8
A_NOPE = 128
A_ROPE = 64
A_VDIM = 128
A_WIDTH = A_HEADS * A_VDIM
Q_LORA = 512
KV_LORA = 256
B_HEADS = 8
B_HDIM = 128
B_WIDTH = B_HEADS * B_HDIM
DILATED_PATTERNS = ((128, 1), (512, 4), (2048, 16))
MIX_WIDTH = A_WIDTH + B_WIDTH
ROPE_THETA = 10000.0
Q_BLOCK = 128
RMS_EPS = 1e-6
LN_EPS = 1e-5
NEG_BIG = -1e30
DEEPNORM_ALPHA = (2 * DEPTH) ** 0.25
DEEPNORM_BETA = (8 * DEPTH) ** -0.25
IN_SPLITS = (Q_LORA, KV_LORA, A_ROPE, B_WIDTH, B_WIDTH, B_WIDTH, A_WIDTH, B_WIDTH)
IN_WIDTH = sum(IN_SPLITS)

kernel_name = 'hybrid_mla_dilated_encoder'


def rms_norm(x, g):
    xf = x.astype(jnp.float32)
    y = xf * lax.rsqrt(jnp.mean(xf * xf, axis=-1, keepdims=True) + RMS_EPS)
    return (y * g.astype(jnp.float32)).astype(x.dtype)


def layer_norm(x, g, b):
    xf = x.astype(jnp.float32)
    mu = jnp.mean(xf, axis=-1, keepdims=True)
    var = jnp.mean(jnp.square(xf - mu), axis=-1, keepdims=True)
    y = (xf - mu) * lax.rsqrt(var + LN_EPS)
    return (y * g.astype(jnp.float32) + b.astype(jnp.float32)).astype(x.dtype)


def rope(x, pos):
    d = x.shape[-1]
    half = d // 2
    inv = ROPE_THETA ** (-jnp.arange(half, dtype=jnp.float32) * 2.0 / d)
    ang = pos.astype(jnp.float32)[:, None] * inv[None, :]
    cos = jnp.cos(ang)[None, :, None, :]
    sin = jnp.sin(ang)[None, :, None, :]
    xf = x.astype(jnp.float32)
    x1, x2 = xf[..., :half], xf[..., half:]
    return jnp.concatenate([x1 * cos - x2 * sin, x2 * cos + x1 * sin], axis=-1).astype(x.dtype)


def mla_attention(c_q, c_kv, k_rope, pos, g_qn, w_uq, g_kvn, w_ukv):
    b, s, _ = c_q.shape
    q = (rms_norm(c_q, g_qn) @ w_uq).reshape(b, s, A_HEADS, A_NOPE + A_ROPE)
    q = jnp.concatenate([q[..., :A_NOPE], rope(q[..., A_NOPE:], pos)], axis=-1)
    kv = (rms_norm(c_kv, g_kvn) @ w_ukv).reshape(b, s, A_HEADS, A_NOPE + A_VDIM)
    k_nope, v = kv[..., :A_NOPE], kv[..., A_NOPE:]
    k_pe = rope(k_rope[:, :, None, :], pos)
    k = jnp.concatenate([k_nope, jnp.broadcast_to(k_pe, (b, s, A_HEADS, A_ROPE))], axis=-1)
    scale = (A_NOPE + A_ROPE) ** -0.5
    nblk = s // Q_BLOCK
    qb = q.reshape(b, nblk, Q_BLOCK, A_HEADS, A_NOPE + A_ROPE).transpose(1, 0, 2, 3, 4)

    def block(qi):
        sc = jnp.einsum('bqhd,bkhd->bhqk', qi, k).astype(jnp.float32) * scale
        p = jax.nn.softmax(sc, axis=-1)
        return jnp.einsum('bhqk,bkhd->bqhd', p.astype(v.dtype), v)

    o = lax.map(block, qb)
    return o.transpose(1, 0, 2, 3, 4).reshape(b, s, A_WIDTH)


def dilated_pattern(q, k, v, window, dilation):
    b, s, h, d = q.shape
    half = window // (2 * dilation)
    chunk = half * dilation
    L = -(-s // chunk) * chunk
    M = L // dilation
    nb = M // half
    scale = d ** -0.5

    def to_blocks(t):
        t = jnp.pad(t, ((0, 0), (0, L - s), (0, 0), (0, 0)))
        t = t.reshape(b, M, dilation, h, d).transpose(0, 2, 1, 3, 4)
        return t.reshape(b, dilation, nb, half, h, d)

    def neighbours(t):
        z = jnp.zeros_like(t[:, :, :1])
        prev = jnp.concatenate([z, t[:, :, :-1]], axis=2)
        nxt = jnp.concatenate([t[:, :, 1:], z], axis=2)
        return jnp.concatenate([prev, t, nxt], axis=3)

    qs = to_blocks(q)
    kn = neighbours(to_blocks(k))
    vn = neighbours(to_blocks(v))
    jq = jnp.arange(nb)[:, None] * half + jnp.arange(half)[None, :]
    jk = jnp.arange(nb)[:, None] * half + jnp.arange(3 * half)[None, :] - half
    orig_k = jk[None] * dilation + jnp.arange(dilation)[:, None, None]
    valid_k = (jk >= 0)[None] & (orig_k < s)
    band = jnp.abs(jq[:, :, None] - jk[:, None, :]) <= half
    mask = band[None] & valid_k[:, :, None, :]
    sc = jnp.einsum('brnqhd,brnkhd->brnhqk', qs, kn).astype(jnp.float32) * scale
    sc = jnp.where(mask[None, :, :, None, :, :], sc, NEG_BIG)
    m = jnp.max(sc, axis=-1)
    p = jnp.exp(sc - m[..., None])
    den = jnp.sum(p, axis=-1)
    num = jnp.einsum('brnhqk,brnkhd->brnqhd', p, vn.astype(jnp.float32))
    num = num.reshape(b, dilation, M, h, d).transpose(0, 2, 1, 3, 4).reshape(b, L, h, d)[:, :s]

    def stat_back(t):
        t = t.transpose(0, 1, 2, 4, 3).reshape(b, dilation, M, h)
        return t.transpose(0, 2, 1, 3).reshape(b, L, h)[:, :s]

    return num, stat_back(m), stat_back(den)


def dilated_attention(q, k, v):
    b, s, h, d = q.shape
    outs = [dilated_pattern(q, k, v, w, dl) for (w, dl) in DILATED_PATTERNS]
    m_all = outs[0][1]
    for o in outs[1:]:
        m_all = jnp.maximum(m_all, o[1])
    num = jnp.zeros((b, s, h, d), jnp.float32)
    den = jnp.zeros((b, s, h), jnp.float32)
    for (n_i, m_i, s_i) in outs:
        wgt = jnp.exp(m_i - m_all)
        num = num + wgt[..., None] * n_i
        den = den + wgt * s_i
    return (num / den[..., None]).astype(q.dtype).reshape(b, s, h * d)


def encoder_layer(x, w_in, g_qn, w_uq, g_kvn, w_ukv, g_oa, g_ob, w_out, ln_g, ln_b):
    b, s, _ = x.shape
    pos = jnp.arange(s)
    hcat = x @ w_in
    idx = [int(i) for i in np.cumsum(IN_SPLITS)[:-1]]
    c_q, c_kv, k_rope, q_b, k_b, v_b, gate_a, gate_b = jnp.split(hcat, idx, axis=-1)
    o_a = mla_attention(c_q, c_kv, k_rope, pos, g_qn, w_uq, g_kvn, w_ukv)
    q_b = rope(q_b.reshape(b, s, B_HEADS, B_HDIM), pos)
    k_b = rope(k_b.reshape(b, s, B_HEADS, B_HDIM), pos)
    v_b = v_b.reshape(b, s, B_HEADS, B_HDIM)
    o_b = dilated_attention(q_b, k_b, v_b)
    o_a = rms_norm(o_a, g_oa) * jax.nn.silu(gate_a)
    o_b = rms_norm(o_b, g_ob) * jax.nn.silu(gate_b)
    out = jnp.concatenate([o_a, o_b], axis=-1) @ w_out
    return layer_norm(DEEPNORM_ALPHA * x + out, ln_g, ln_b)


def setup_inputs(seed: int = 0) -> dict:
    key = jax.random.key(seed)
    ks = jax.random.split(key, 13)
    f32 = jnp.float32
    x_prompt = jax.random.normal(ks[0], (BATCH, SEQ, D_MODEL), f32)
    x_sample = jax.random.normal(ks[1], (DEC_BATCH, DEC_SEQ, D_MODEL), f32)
    col_scale = jnp.concatenate([
        jnp.ones((Q_LORA + KV_LORA + A_ROPE + 2 * B_WIDTH,), f32),
        jnp.full((B_WIDTH,), DEEPNORM_BETA, f32),
        jnp.ones((A_WIDTH + B_WIDTH,), f32)])
    w_in = jax.random.normal(ks[2], (DEPTH, D_MODEL, IN_WIDTH), f32) * (D_MODEL ** -0.5) * col_scale
    g_qn = 1.0 + 0.02 * jax.random.normal(ks[3], (DEPTH, Q_LORA), f32)
    w_uq = jax.random.normal(ks[4], (DEPTH, Q_LORA, A_HEADS * (A_NOPE + A_ROPE)), f32) * (Q_LORA ** -0.5)
    g_kvn = 1.0 + 0.02 * jax.random.normal(ks[5], (DEPTH, KV_LORA), f32)
    kv_scale = jnp.tile(jnp.concatenate([jnp.ones((A_NOPE,), f32), jnp.full((A_VDIM,), DEEPNORM_BETA, f32)]), A_HEADS)
    w_ukv = jax.random.normal(ks[6], (DEPTH, KV_LORA, A_HEADS * (A_NOPE + A_VDIM)), f32) * (KV_LORA ** -0.5) * kv_scale
    g_oa = 1.0 + 0.02 * jax.random.normal(ks[7], (DEPTH, A_WIDTH), f32)
    g_ob = 1.0 + 0.02 * jax.random.normal(ks[8], (DEPTH, B_WIDTH), f32)
    w_out = jax.random.normal(ks[9], (DEPTH, MIX_WIDTH, D_MODEL), f32) * (MIX_WIDTH ** -0.5) * DEEPNORM_BETA
    ln_g = 1.0 + 0.02 * jax.random.normal(ks[10], (DEPTH, D_MODEL), f32)
    ln_b = 0.02 * jax.random.normal(ks[11], (DEPTH, D_MODEL), f32)
    return {'x_prompt': x_prompt, 'x_sample': x_sample, 'w_in': w_in, 'g_qn': g_qn, 'w_uq': w_uq,
            'g_kvn': g_kvn, 'w_ukv': w_ukv, 'g_oa': g_oa, 'g_ob': g_ob, 'w_out': w_out,
            'ln_g': ln_g, 'ln_b': ln_b}


def reference(x_prompt, x_sample, w_in, g_qn, w_uq, g_kvn, w_ukv, g_oa, g_ob, w_out, ln_g, ln_b):
    def trunk(x):
        for l in range(DEPTH):
            x = encoder_layer(x, w_in[l], g_qn[l], w_uq[l], g_kvn[l], w_ukv[l],
                              g_oa[l], g_ob[l], w_out[l], ln_g[l], ln_b[l])
        return x
    y_prompt = trunk(x_prompt)
    y_sample = trunk(x_sample)
    return (y_prompt, y_sample)
```

```python
import functools

import jax
import jax.numpy as jnp
import numpy as np
from jax import lax
from jax.experimental import pallas as pl
from jax.experimental.pallas import tpu as pltpu

F32 = jnp.float32
BF16 = jnp.bfloat16

D_MODEL = 2048
DEPTH = 1
A_HEADS = 8
A_NOPE = 128
A_ROPE = 64
A_VDIM = 128
A_QK = A_NOPE + A_ROPE
A_WIDTH = A_HEADS * A_VDIM
Q_LORA = 512
KV_LORA = 256
B_HEADS = 8
B_HDIM = 128
B_WIDTH = B_HEADS * B_HDIM
DILATED_PATTERNS = ((128, 1), (512, 4), (2048, 16))
ROPE_THETA = 10000.0
RMS_EPS = 1e-6
LN_EPS = 1e-5
NEG_BIG = -1e30
DEEPNORM_ALPHA = (2 * DEPTH) ** 0.25
A_SCALE = A_QK ** -0.5
B_SCALE = B_HDIM ** -0.5

LANES = 128
SEG = 1024
N_SEG = 6
LAT_PAD = SEG - (Q_LORA + KV_LORA + A_ROPE)
HALF_WIN = 64
VMEM_LIMIT = 56 * 1024 * 1024


def _cparams(semantics):
    return pltpu.CompilerParams(dimension_semantics=semantics, vmem_limit_bytes=VMEM_LIMIT)


def _nt_dot(a, b):
    return lax.dot_general(a, b, (((1,), (1,)), ((), ())), preferred_element_type=F32)


def _in_proj_kernel(x_ref, w_ref, cos_ref, sin_ref,
                    lat_ref, qb_ref, kb_ref, vb_ref, ga_ref, gb_ref, xb_ref):
    j = pl.program_id(1)

    @pl.when(j == 0)
    def _():
        xb_ref[...] = x_ref[...].astype(BF16)

    acc = jnp.dot(xb_ref[...], w_ref[...], preferred_element_type=F32)

    def rope_store(o_ref, scale):
        cos = cos_ref[...]
        sin = sin_ref[...]
        for h in range(B_HEADS):
            hs = slice(h * B_HDIM, (h + 1) * B_HDIM)
            xh = acc[:, hs]
            r = xh * cos + pltpu.roll(xh, B_HDIM // 2, 1) * sin
            o_ref[:, hs] = (r * scale).astype(o_ref.dtype)

    def silu_store(o_ref):
        o_ref[...] = (acc * jax.nn.sigmoid(acc)).astype(o_ref.dtype)

    @pl.when(j == 0)
    def _():
        lat_ref[...] = acc

    @pl.when(j == 1)
    def _():
        rope_store(qb_ref, B_SCALE)

    @pl.when(j == 2)
    def _():
        rope_store(kb_ref, 1.0)

    @pl.when(j == 3)
    def _():
        vb_ref[...] = acc.astype(vb_ref.dtype)

    @pl.when(j == 4)
    def _():
        silu_store(ga_ref)

    @pl.when(j == 5)
    def _():
        silu_store(gb_ref)


def _in_proj(x2, w_seg, cos_b, sin_b, s, tm):
    n = x2.shape[0]
    n_pos = s // tm
    row = lambda i, j: (i, 0)
    out_bf = jax.ShapeDtypeStruct((n, SEG), BF16)
    return pl.pallas_call(
        _in_proj_kernel,
        grid=(n // tm, N_SEG),
        in_specs=[
            pl.BlockSpec((tm, D_MODEL), row),
            pl.BlockSpec((D_MODEL, SEG), lambda i, j: (0, j)),
            pl.BlockSpec((tm, B_HDIM), lambda i, j: (i % n_pos, 0)),
            pl.BlockSpec((tm, B_HDIM), lambda i, j: (i % n_pos, 0)),
        ],
        out_specs=[pl.BlockSpec((tm, SEG), row)] * N_SEG,
        out_shape=[jax.ShapeDtypeStruct((n, SEG), F32)] + [out_bf] * (N_SEG - 1),
        scratch_shapes=[pltpu.VMEM((tm, D_MODEL), BF16)],
        compiler_params=_cparams(("parallel", "arbitrary")),
        name="in_proj",
    )(x2, w_seg, cos_b, sin_b)


def _rms(x, g):
    return x * lax.rsqrt(jnp.mean(x * x, axis=-1, keepdims=True) + RMS_EPS) * g


def _mla_up_kernel(lat_ref, gq_ref, gkv_ref, wuqt_ref, wuk_ref, wuvt_ref,
                   cosk_ref, sink_ref, cost_ref, sint_ref, qt_ref, k_ref, vt_ref):
    lat = lat_ref[0]
    cqn = _rms(lat[:, :Q_LORA], gq_ref[...]).astype(BF16)
    ckvn = _rms(lat[:, Q_LORA:Q_LORA + KV_LORA], gkv_ref[...]).astype(BF16)
    kr = lat[:, Q_LORA + KV_LORA:Q_LORA + KV_LORA + LANES]
    lane = lax.broadcasted_iota(jnp.int32, kr.shape, 1)
    half = A_ROPE // 2
    rot = jnp.where(lane < half, pltpu.roll(kr, LANES - half, 1), pltpu.roll(kr, half, 1))
    kpe = (kr * cosk_ref[...] + rot * sink_ref[...]).astype(BF16)
    ct = cost_ref[...]
    st = sint_ref[...]
    for h in range(A_HEADS):
        qt = _nt_dot(wuqt_ref[h], cqn)
        x1 = qt[A_NOPE:A_NOPE + half]
        x2 = qt[A_NOPE + half:]
        qt_ref[0, h, 0:A_NOPE, :] = (qt[:A_NOPE] * A_SCALE).astype(BF16)
        qt_ref[0, h, A_NOPE:A_NOPE + half, :] = ((x1 * ct - x2 * st) * A_SCALE).astype(BF16)
        qt_ref[0, h, A_NOPE + half:, :] = ((x2 * ct + x1 * st) * A_SCALE).astype(BF16)
        k_ref[0, h, :, 0:A_NOPE] = jnp.dot(ckvn, wuk_ref[h], preferred_element_type=F32).astype(BF16)
        k_ref[0, h, :, A_NOPE:] = kpe[:, :A_ROPE]
        vt_ref[0, h] = _nt_dot(wuvt_ref[h], ckvn).astype(BF16)


def _mla_up(lat3, gq, gkv, wuqt, wuk, wuvt, cosk, sink, cost, sint, tm):
    b, s, _ = lat3.shape
    full = lambda *shape: pl.BlockSpec(shape, lambda bi, si: (0,) * len(shape))
    return pl.pallas_call(
        _mla_up_kernel,
        grid=(b, s // tm),
        in_specs=[
            pl.BlockSpec((1, tm, SEG), lambda bi, si: (bi, si, 0)),
            full(1, Q_LORA), full(1, KV_LORA),
            full(A_HEADS, A_QK, Q_LORA), full(A_HEADS, KV_LORA, A_NOPE), full(A_HEADS, A_VDIM, KV_LORA),
            pl.BlockSpec((tm, LANES), lambda bi, si: (si, 0)),
            pl.BlockSpec((tm, LANES), lambda bi, si: (si, 0)),
            pl.BlockSpec((A_ROPE // 2, tm), lambda bi, si: (0, si)),
            pl.BlockSpec((A_ROPE // 2, tm), lambda bi, si: (0, si)),
        ],
        out_specs=[
            pl.BlockSpec((1, A_HEADS, A_QK, tm), lambda bi, si: (bi, 0, 0, si)),
            pl.BlockSpec((1, A_HEADS, tm, A_QK), lambda bi, si: (bi, 0, si, 0)),
            pl.BlockSpec((1, A_HEADS, A_VDIM, tm), lambda bi, si: (bi, 0, 0, si)),
        ],
        out_shape=[
            jax.ShapeDtypeStruct((b, A_HEADS, A_QK, s), BF16),
            jax.ShapeDtypeStruct((b, A_HEADS, s, A_QK), BF16),
            jax.ShapeDtypeStruct((b, A_HEADS, A_VDIM, s), BF16),
        ],
        compiler_params=_cparams(("parallel", "parallel")),
        name="mla_up",
    )(lat3, gq, gkv, wuqt, wuk, wuvt, cosk, sink, cost, sint)


def _mla_flash_kernel(qt_ref, k_ref, vt_ref, o_ref, m_ref, l_ref, acc_ref, *, bk, nk):
    qt = qt_ref[0, 0]
    m_ref[...] = jnp.full(m_ref.shape, NEG_BIG, F32)
    l_ref[...] = jnp.zeros(l_ref.shape, F32)
    acc_ref[...] = jnp.zeros(acc_ref.shape, F32)

    def body(t, carry):
        start = pl.multiple_of(t * bk, bk)
        k = k_ref[0, 0, pl.ds(start, bk), :]
        st = jnp.dot(k, qt, preferred_element_type=F32)
        m_prev = m_ref[...]
        m_new = jnp.maximum(m_prev, jnp.max(st, axis=0, keepdims=True))
        alpha = jnp.exp(m_prev - m_new)
        pt = jnp.exp(st - m_new)
        l_ref[...] = alpha * l_ref[...] + jnp.sum(pt, axis=0, keepdims=True)
        vt = vt_ref[0, 0, :, pl.ds(start, bk)]
        acc_ref[...] = alpha * acc_ref[...] + jnp.dot(vt, pt.astype(BF16), preferred_element_type=F32)
        m_ref[...] = m_new
        return carry

    lax.fori_loop(0, nk, body, 0)
    o = acc_ref[...] * (1.0 / l_ref[...])
    o_ref[0] = o.T


def _mla_flash(qt, k, vt, bq, bk):
    b, h, _, s = qt.shape
    kern = functools.partial(_mla_flash_kernel, bk=bk, nk=s // bk)
    return pl.pallas_call(
        kern,
        grid=(b, h, s // bq),
        in_specs=[
            pl.BlockSpec((1, 1, A_QK, bq), lambda bi, hi, qi: (bi, hi, 0, qi)),
            pl.BlockSpec((1, 1, s, A_QK), lambda bi, hi, qi: (bi, hi, 0, 0)),
            pl.BlockSpec((1, 1, A_VDIM, s), lambda bi, hi, qi: (bi, hi, 0, 0)),
        ],
        out_specs=pl.BlockSpec((1, bq, A_VDIM), lambda bi, hi, qi: (bi, qi, hi)),
        out_shape=jax.ShapeDtypeStruct((b, s, A_WIDTH), F32),
        scratch_shapes=[pltpu.VMEM((1, bq), F32), pltpu.VMEM((1, bq), F32), pltpu.VMEM((A_VDIM, bq), F32)],
        compiler_params=_cparams(("parallel", "parallel", "arbitrary")),
        name="mla_flash",
    )(qt, k, vt)


def _dilated_kernel(q_ref, kp_ref, kc_ref, kn_ref, vp_ref, vc_ref, vn_ref, o_ref, lse_ref, *, bq, nblk):
    i = pl.program_id(2)
    nkeys = bq + 2 * HALF_WIN
    a = lax.broadcasted_iota(jnp.int32, (bq, nkeys), 0)
    c = lax.broadcasted_iota(jnp.int32, (bq, nkeys), 1)
    band = (c >= a) & (c <= a + 2 * HALF_WIN)
    valid = ((c >= HALF_WIN) | (i > 0)) & ((c < bq + HALF_WIN) | (i < nblk - 1))
    mask = band & valid
    lane = lax.broadcasted_iota(jnp.int32, (bq, LANES), 1)
    lse_all = jnp.zeros((bq, LANES), F32)
    for h in range(B_HEADS):
        hs = slice(h * B_HDIM, (h + 1) * B_HDIM)
        q = q_ref[0, :, hs]
        k = jnp.concatenate([kp_ref[0, :, hs], kc_ref[0, :, hs], kn_ref[0, :, hs]], axis=0)
        v = jnp.concatenate([vp_ref[0, :, hs], vc_ref[0, :, hs], vn_ref[0, :, hs]], axis=0)
        sc = jnp.where(mask, _nt_dot(q, k), NEG_BIG)
        m = jnp.max(sc, axis=1, keepdims=True)
        p = jnp.exp(sc - m)
        den = jnp.sum(p, axis=1, keepdims=True)
        num = jnp.dot(p.astype(BF16), v, preferred_element_type=F32)
        o_ref[0, :, hs] = num * (1.0 / den)
        lse_all = jnp.where(lane == h, m + jnp.log(den), lse_all)
    lse_ref[0] = lse_all


def _dilated(qb, kb, vb, dil, bq):
    b, s, _ = qb.shape
    m_len = s // dil
    nblk = m_len // bq
    sub = bq // HALF_WIN
    n_half = m_len // HALF_WIN
    view = lambda t: t.reshape(b, m_len, dil * B_WIDTH)
    cur = pl.BlockSpec((1, bq, B_WIDTH), lambda bi, r, i: (bi, i, r))
    prev = pl.BlockSpec((1, HALF_WIN, B_WIDTH), lambda bi, r, i: (bi, jnp.maximum(i * sub - 1, 0), r))
    nxt = pl.BlockSpec((1, HALF_WIN, B_WIDTH), lambda bi, r, i: (bi, jnp.minimum((i + 1) * sub, n_half - 1), r))
    kern = functools.partial(_dilated_kernel, bq=bq, nblk=nblk)
    o, lse = pl.pallas_call(
        kern,
        grid=(b, dil, nblk),
        in_specs=[cur, prev, cur, nxt, prev, cur, nxt],
        out_specs=[cur, pl.BlockSpec((1, bq, LANES), lambda bi, r, i: (bi, i, r))],
        out_shape=[jax.ShapeDtypeStruct((b, m_len, dil * B_WIDTH), F32),
                   jax.ShapeDtypeStruct((b, m_len, dil * LANES), F32)],
        compiler_params=_cparams(("parallel", "parallel", "parallel")),
        name=f"dilated_d{dil}",
    )(view(qb), view(kb), view(kb), view(kb), view(vb), view(vb), view(vb))
    return o.reshape(b * s, B_WIDTH), lse.reshape(b * s, LANES)


def _out_proj_kernel(oa_ref, o1_ref, o2_ref, o3_ref, l1_ref, l2_ref, l3_ref, ga_ref, gb_ref, x_ref,
                     w_ref, goa_ref, gob_ref, lng_ref, lnb_ref, y_ref):
    za = (_rms(oa_ref[...], goa_ref[...]) * ga_ref[...].astype(F32)).astype(BF16)
    l1, l2, l3 = l1_ref[...], l2_ref[...], l3_ref[...]
    mx = jnp.maximum(jnp.maximum(l1, l2), l3)
    e1, e2, e3 = jnp.exp(l1 - mx), jnp.exp(l2 - mx), jnp.exp(l3 - mx)
    inv = 1.0 / (e1 + e2 + e3)
    w1, w2, w3 = e1 * inv, e2 * inv, e3 * inv
    parts = []
    for h in range(B_HEADS):
        hs = slice(h * B_HDIM, (h + 1) * B_HDIM)
        parts.append(w1[:, h:h + 1] * o1_ref[:, hs] + w2[:, h:h + 1] * o2_ref[:, hs]
                     + w3[:, h:h + 1] * o3_ref[:, hs])
    ob = jnp.concatenate(parts, axis=1)
    zb = (_rms(ob, gob_ref[...]) * gb_ref[...].astype(F32)).astype(BF16)
    out = (jnp.dot(za, w_ref[:A_WIDTH, :], preferred_element_type=F32)
           + jnp.dot(zb, w_ref[A_WIDTH:, :], preferred_element_type=F32))
    r = DEEPNORM_ALPHA * x_ref[...] + out
    mu = jnp.mean(r, axis=-1, keepdims=True)
    d = r - mu
    var = jnp.mean(d * d, axis=-1, keepdims=True)
    y_ref[...] = d * lax.rsqrt(var + LN_EPS) * lng_ref[...] + lnb_ref[...]


def _out_proj(oa, obs, lses, ga, gb, x2, w_out, goa, gob, lng, lnb, tm):
    n = x2.shape[0]
    row = lambda w: pl.BlockSpec((tm, w), lambda i: (i, 0))
    full = lambda *shape: pl.BlockSpec(shape, lambda i: (0,) * len(shape))
    return pl.pallas_call(
        _out_proj_kernel,
        grid=(n // tm,),
        in_specs=[row(A_WIDTH)] + [row(B_WIDTH)] * 3 + [row(LANES)] * 3 + [row(A_WIDTH), row(B_WIDTH), row(D_MODEL),
                  full(A_WIDTH + B_WIDTH, D_MODEL), full(1, A_WIDTH), full(1, B_WIDTH),
                  full(1, D_MODEL), full(1, D_MODEL)],
        out_specs=row(D_MODEL),
        out_shape=jax.ShapeDtypeStruct((n, D_MODEL), F32),
        compiler_params=_cparams(("parallel",)),
        name="out_proj",
    )(oa, *obs, *lses, ga, gb, x2, w_out, goa, gob, lng, lnb)


def _rope_tables(s):
    pos = jnp.arange(s, dtype=F32)

    def cos_sin(d):
        half = d // 2
        inv = ROPE_THETA ** (-jnp.arange(half, dtype=F32) * 2.0 / d)
        ang = pos[:, None] * inv[None, :]
        return jnp.cos(ang), jnp.sin(ang)

    cb, sb = cos_sin(B_HDIM)
    cos_b = jnp.concatenate([cb, cb], axis=1)
    sin_b = jnp.concatenate([-sb, sb], axis=1)
    ca, sa = cos_sin(A_ROPE)
    zeros = jnp.zeros((s, LANES - A_ROPE), F32)
    cos_k = jnp.concatenate([ca, ca, zeros], axis=1)
    sin_k = jnp.concatenate([-sa, sa, zeros], axis=1)
    return cos_b, sin_b, cos_k, sin_k, ca.T, sa.T


def _prep_weights(w_in, g_qn, w_uq, g_kvn, w_ukv, g_oa, g_ob, w_out, ln_g, ln_b):
    n_lat = Q_LORA + KV_LORA + A_ROPE
    w_seg = jnp.concatenate([w_in[:, :n_lat], jnp.zeros((D_MODEL, LAT_PAD), w_in.dtype), w_in[:, n_lat:]],
                            axis=1).astype(BF16)
    wuqt = w_uq.reshape(Q_LORA, A_HEADS, A_QK).transpose(1, 2, 0).astype(BF16)
    wukv = w_ukv.reshape(KV_LORA, A_HEADS, A_NOPE + A_VDIM)
    wuk = wukv[:, :, :A_NOPE].transpose(1, 0, 2).astype(BF16)
    wuvt = wukv[:, :, A_NOPE:].transpose(1, 2, 0).astype(BF16)
    return dict(w_seg=w_seg, gq=g_qn[None, :], gkv=g_kvn[None, :], wuqt=wuqt, wuk=wuk, wuvt=wuvt,
                goa=g_oa[None, :], gob=g_ob[None, :], w_out=w_out.astype(BF16),
                lng=ln_g[None, :], lnb=ln_b[None, :])


def _tiles(s):
    pick = lambda pref, n: pref if n % pref == 0 else n
    return dict(tm_in=pick(512, s), tm_up=pick(512, s), bq=pick(512, s), bk=pick(512, s), tm_out=pick(256, s))


def _encoder_layer(x, p):
    b, s, _ = x.shape
    longest = max(w // 2 for w, _ in DILATED_PATTERNS)
    assert s % longest == 0, "sequence length must be a multiple of the largest dilated chunk"
    t = _tiles(s)
    cos_b, sin_b, cos_k, sin_k, cos_t, sin_t = _rope_tables(s)
    x2 = x.reshape(b * s, D_MODEL)
    lat, qb, kb, vb, ga, gb = _in_proj(x2, p["w_seg"], cos_b, sin_b, s, t["tm_in"])
    qt, k, vt = _mla_up(lat.reshape(b, s, SEG), p["gq"], p["gkv"], p["wuqt"], p["wuk"], p["wuvt"],
                        cos_k, sin_k, cos_t, sin_t, t["tm_up"])
    oa = _mla_flash(qt, k, vt, t["bq"], t["bk"]).reshape(b * s, A_WIDTH)
    obs, lses = [], []
    for window, dil in DILATED_PATTERNS:
        assert window // (2 * dil) == HALF_WIN
        m_len = s // dil
        o, lse = _dilated(qb.reshape(b, s, B_WIDTH), kb.reshape(b, s, B_WIDTH), vb.reshape(b, s, B_WIDTH),
                          dil, min(256, m_len))
        obs.append(o)
        lses.append(lse)
    y = _out_proj(oa, obs, lses, ga, gb, x2, p["w_out"], p["goa"], p["gob"], p["lng"], p["lnb"], t["tm_out"])
    return y.reshape(b, s, D_MODEL)


def kernel(x_prompt, x_sample, w_in, g_qn, w_uq, g_kvn, w_ukv, g_oa, g_ob, w_out, ln_g, ln_b):
    def trunk(x):
        for l in range(DEPTH):
            p = _prep_weights(w_in[l], g_qn[l], w_uq[l], g_kvn[l], w_ukv[l], g_oa[l], g_ob[l],
                              w_out[l], ln_g[l], ln_b[l])
            x = _encoder_layer(x, p)
        return x

    return (trunk(x_prompt), trunk(x_sample))
```

```python
import functools

import jax
import jax.numpy as jnp
from jax import lax
from jax.experimental import pallas as pl
from jax.experimental.pallas import tpu as pltpu

F32 = jnp.float32
BF16 = jnp.bfloat16

D_MODEL = 2048
DEPTH = 1
A_HEADS = 8
A_NOPE = 128
A_ROPE = 64
A_VDIM = 128
A_QK = A_NOPE + A_ROPE
A_WIDTH = A_HEADS * A_VDIM
Q_LORA = 512
KV_LORA = 256
B_HEADS = 8
B_HDIM = 128
B_WIDTH = B_HEADS * B_HDIM
DILATED_PATTERNS = ((128, 1), (512, 4), (2048, 16))
ROPE_THETA = 10000.0
RMS_EPS = 1e-6
LN_EPS = 1e-5
NEG_BIG = -1e30
DEEPNORM_ALPHA = (2 * DEPTH) ** 0.25
LOG2E = 1.4426950408889634
A_SCALE = A_QK ** -0.5 * LOG2E
B_SCALE = B_HDIM ** -0.5

LANES = 128
SEG = 1024
N_SEG = 6
LAT_PAD = SEG - (Q_LORA + KV_LORA + A_ROPE)
HALF_WIN = 64
VMEM_LIMIT = 56 * 1024 * 1024


def _cparams(semantics):
    return pltpu.CompilerParams(dimension_semantics=semantics, vmem_limit_bytes=VMEM_LIMIT)


def _nt_dot(a, b):
    return lax.dot_general(a, b, (((1,), (1,)), ((), ())), preferred_element_type=F32)


def _in_proj_kernel(x_ref, w_ref, *refs, mode, scale):
    acc = jnp.dot(x_ref[...].astype(BF16), w_ref[...], preferred_element_type=F32)
    if mode == "lat":
        refs[0][...] = acc
        return
    if mode == "silu":
        refs[0][...] = (acc * jax.nn.sigmoid(acc)).astype(BF16)
        return
    if mode == "rope":
        cos_ref, sin_ref = refs[:2]
        refs = refs[2:]
    outs, seg_ref = refs[:-1], refs[-1]
    if mode == "rope":
        cos = cos_ref[...]
        sin = sin_ref[...]
    tm = seg_ref.shape[1]
    for h in range(B_HEADS):
        xh = acc[:, h * B_HDIM:(h + 1) * B_HDIM]
        if mode == "rope":
            xh = (xh * cos + pltpu.roll(xh, B_HDIM // 2, 1) * sin) * scale
        seg_ref[h] = xh
        for o_ref, (_, dil) in zip(outs, DILATED_PATTERNS):
            for r in range(dil):
                rows = pl.ds(r, tm // dil, stride=dil) if dil > 1 else slice(None)
                col = r * SEG + h * B_HDIM
                o_ref[:, col:col + B_HDIM] = seg_ref[h, rows, :].astype(BF16)


def _in_proj(x2, w_seg, seg, mode, tm, s=None, cos_b=None, sin_b=None, scale=1.0):
    n = x2.shape[0]
    row = lambda i: (i, 0)
    in_specs = [pl.BlockSpec((tm, D_MODEL), row), pl.BlockSpec((D_MODEL, SEG), lambda i: (0, seg))]
    args = [x2, w_seg]
    scratch = []
    if mode == "rope":
        n_pos = s // tm
        in_specs += [pl.BlockSpec((tm, B_HDIM), lambda i: (i % n_pos, 0))] * 2
        args += [cos_b, sin_b]
    if mode in ("lat", "silu"):
        out_specs = pl.BlockSpec((tm, SEG), row)
        out_shape = jax.ShapeDtypeStruct((n, SEG), F32 if mode == "lat" else BF16)
    else:
        out_specs = [pl.BlockSpec((tm // d, d * SEG), row) for _, d in DILATED_PATTERNS]
        out_shape = [jax.ShapeDtypeStruct((n // d, d * SEG), BF16) for _, d in DILATED_PATTERNS]
        scratch = [pltpu.VMEM((B_HEADS, tm, B_HDIM), F32)]
    return pl.pallas_call(
        functools.partial(_in_proj_kernel, mode=mode, scale=scale),
        grid=(n // tm,),
        in_specs=in_specs,
        out_specs=out_specs,
        out_shape=out_shape,
        scratch_shapes=scratch,
        compiler_params=_cparams(("parallel",)),
        name=f"in_proj_{seg}_{mode}",
    )(*args)


def _rms(x, g):
    return x * lax.rsqrt(jnp.mean(x * x, axis=-1, keepdims=True) + RMS_EPS) * g


def _mla_up_kernel(lat_ref, gq_ref, gkv_ref, wuqt_ref, wuk_ref, wuvt_ref,
                   cosk_ref, sink_ref, cost_ref, sint_ref, qt_ref, k_ref, vt_ref):
    lat = lat_ref[0]
    cqn = _rms(lat[:, :Q_LORA], gq_ref[...]).astype(BF16)
    ckvn = _rms(lat[:, Q_LORA:Q_LORA + KV_LORA], gkv_ref[...]).astype(BF16)
    kr = lat[:, Q_LORA + KV_LORA:Q_LORA + KV_LORA + LANES]
    lane = lax.broadcasted_iota(jnp.int32, kr.shape, 1)
    half = A_ROPE // 2
    rot = jnp.where(lane < half, pltpu.roll(kr, LANES - half, 1), pltpu.roll(kr, half, 1))
    kpe = (kr * cosk_ref[...] + rot * sink_ref[...]).astype(BF16)
    ct = cost_ref[...]
    st = sint_ref[...]
    for h in range(A_HEADS):
        qt = _nt_dot(wuqt_ref[h], cqn)
        x1 = qt[A_NOPE:A_NOPE + half]
        x2 = qt[A_NOPE + half:]
        qt_ref[0, h, 0:A_NOPE, :] = (qt[:A_NOPE] * A_SCALE).astype(BF16)
        qt_ref[0, h, A_NOPE:A_NOPE + half, :] = ((x1 * ct - x2 * st) * A_SCALE).astype(BF16)
        qt_ref[0, h, A_NOPE + half:, :] = ((x2 * ct + x1 * st) * A_SCALE).astype(BF16)
        k_ref[0, h, :, 0:A_NOPE] = jnp.dot(ckvn, wuk_ref[h], preferred_element_type=F32).astype(BF16)
        k_ref[0, h, :, A_NOPE:] = kpe[:, :A_ROPE]
        vt_ref[0, h] = _nt_dot(wuvt_ref[h], ckvn).astype(BF16)


def _mla_up(lat3, gq, gkv, wuqt, wuk, wuvt, cosk, sink, cost, sint, tm):
    b, s, _ = lat3.shape
    full = lambda *shape: pl.BlockSpec(shape, lambda bi, si: (0,) * len(shape))
    return pl.pallas_call(
        _mla_up_kernel,
        grid=(b, s // tm),
        in_specs=[
            pl.BlockSpec((1, tm, SEG), lambda bi, si: (bi, si, 0)),
            full(1, Q_LORA), full(1, KV_LORA),
            full(A_HEADS, A_QK, Q_LORA), full(A_HEADS, KV_LORA, A_NOPE), full(A_HEADS, A_VDIM, KV_LORA),
            pl.BlockSpec((tm, LANES), lambda bi, si: (si, 0)),
            pl.BlockSpec((tm, LANES), lambda bi, si: (si, 0)),
            pl.BlockSpec((A_ROPE // 2, tm), lambda bi, si: (0, si)),
            pl.BlockSpec((A_ROPE // 2, tm), lambda bi, si: (0, si)),
        ],
        out_specs=[
            pl.BlockSpec((1, A_HEADS, A_QK, tm), lambda bi, si: (bi, 0, 0, si)),
            pl.BlockSpec((1, A_HEADS, tm, A_QK), lambda bi, si: (bi, 0, si, 0)),
            pl.BlockSpec((1, A_HEADS, A_VDIM, tm), lambda bi, si: (bi, 0, 0, si)),
        ],
        out_shape=[
            jax.ShapeDtypeStruct((b, A_HEADS, A_QK, s), BF16),
            jax.ShapeDtypeStruct((b, A_HEADS, s, A_QK), BF16),
            jax.ShapeDtypeStruct((b, A_HEADS, A_VDIM, s), BF16),
        ],
        compiler_params=_cparams(("parallel", "parallel")),
        name="mla_up",
    )(lat3, gq, gkv, wuqt, wuk, wuvt, cosk, sink, cost, sint)


def _mla_flash_kernel(qt_ref, k_ref, vt_ref, o_ref, s0_ref, s1_ref, p0_ref, p1_ref, acc_ref, *, bk, nk):
    qt = qt_ref[0, 0]
    bq = qt.shape[1]

    def scores(t, s_ref):
        start = pl.multiple_of(t * bk, bk)
        s_ref[...] = jnp.dot(k_ref[0, 0, pl.ds(start, bk), :], qt, preferred_element_type=F32)

    def softmax(s_ref, p_ref, m, l):
        st = s_ref[...]
        m_new = jnp.maximum(m, jnp.max(st, axis=0, keepdims=True))
        alpha = jnp.exp2(m - m_new)
        pt = jnp.exp2(st - m_new)
        p_ref[...] = pt.astype(BF16)
        return m_new, alpha * l + jnp.sum(pt, axis=0, keepdims=True), alpha

    def values(t, p_ref, alpha):
        start = pl.multiple_of(t * bk, bk)
        vt = vt_ref[0, 0, :, pl.ds(start, bk)]
        acc_ref[...] = alpha * acc_ref[...] + jnp.dot(vt, p_ref[...], preferred_element_type=F32)

    acc_ref[...] = jnp.zeros(acc_ref.shape, F32)
    m = jnp.full((1, bq), NEG_BIG, F32)
    l = jnp.zeros((1, bq), F32)
    scores(0, s0_ref)
    scores(1, s1_ref)
    m, l, alpha = softmax(s0_ref, p0_ref, m, l)

    def pair(j, carry):
        m, l, alpha = carry
        t = 2 * j + 1
        scores(t + 1, s0_ref)
        m, l, alpha_t = softmax(s1_ref, p1_ref, m, l)
        values(t - 1, p0_ref, alpha)
        scores(t + 2, s1_ref)
        m, l, alpha_u = softmax(s0_ref, p0_ref, m, l)
        values(t, p1_ref, alpha_t)
        return m, l, alpha_u

    m, l, alpha = lax.fori_loop(0, (nk - 2) // 2, pair, (m, l, alpha))
    m, l, alpha_last = softmax(s1_ref, p1_ref, m, l)
    values(nk - 2, p0_ref, alpha)
    values(nk - 1, p1_ref, alpha_last)
    o = acc_ref[...] * (1.0 / l)
    o_ref[0] = o.T


def _mla_flash(qt, k, vt, bq, bk):
    b, h, _, s = qt.shape
    nk = s // bk
    assert nk >= 2 and nk % 2 == 0
    kern = functools.partial(_mla_flash_kernel, bk=bk, nk=nk)
    return pl.pallas_call(
        kern,
        grid=(b, h, s // bq),
        in_specs=[
            pl.BlockSpec((1, 1, A_QK, bq), lambda bi, hi, qi: (bi, hi, 0, qi)),
            pl.BlockSpec((1, 1, s, A_QK), lambda bi, hi, qi: (bi, hi, 0, 0)),
            pl.BlockSpec((1, 1, A_VDIM, s), lambda bi, hi, qi: (bi, hi, 0, 0)),
        ],
        out_specs=pl.BlockSpec((1, bq, A_VDIM), lambda bi, hi, qi: (bi, qi, hi)),
        out_shape=jax.ShapeDtypeStruct((b, s, A_WIDTH), F32),
        scratch_shapes=[pltpu.VMEM((bk, bq), F32), pltpu.VMEM((bk, bq), F32),
                        pltpu.VMEM((bk, bq), BF16), pltpu.VMEM((bk, bq), BF16),
                        pltpu.VMEM((A_VDIM, bq), F32)],
        compiler_params=_cparams(("parallel", "parallel", "arbitrary")),
        name="mla_flash",
    )(qt, k, vt)


def _dilated_kernel(q_ref, kp_ref, kc_ref, kn_ref, vp_ref, vc_ref, vn_ref, o_ref, lse_ref, *, bq, nblk):
    i = pl.program_id(2)
    nkeys = bq + 2 * HALF_WIN
    a = lax.broadcasted_iota(jnp.int32, (bq, nkeys), 0)
    c = lax.broadcasted_iota(jnp.int32, (bq, nkeys), 1)
    band = (c >= a) & (c <= a + 2 * HALF_WIN)
    valid = ((c >= HALF_WIN) | (i > 0)) & ((c < bq + HALF_WIN) | (i < nblk - 1))
    mask = band & valid
    lane = lax.broadcasted_iota(jnp.int32, (bq, LANES), 1)
    lse_all = jnp.zeros((bq, LANES), F32)
    for h in range(B_HEADS):
        hs = slice(h * B_HDIM, (h + 1) * B_HDIM)
        q = q_ref[0, :, hs]
        k = jnp.concatenate([kp_ref[0, :, hs], kc_ref[0, :, hs], kn_ref[0, :, hs]], axis=0)
        v = jnp.concatenate([vp_ref[0, :, hs], vc_ref[0, :, hs], vn_ref[0, :, hs]], axis=0)
        sc = jnp.where(mask, _nt_dot(q, k), NEG_BIG)
        m = jnp.max(sc, axis=1, keepdims=True)
        p = jnp.exp(sc - m)
        den = jnp.sum(p, axis=1, keepdims=True)
        num = jnp.dot(p.astype(BF16), v, preferred_element_type=F32)
        o_ref[0, :, hs] = num * (1.0 / den)
        lse_all = jnp.where(lane == h, m + jnp.log(den), lse_all)
    lse_ref[0] = lse_all


def _dilated(qv, kv, vv, b, dil, bq):
    m_len = qv.shape[0] // b
    nblk = m_len // bq
    sub = bq // HALF_WIN
    n_half = m_len // HALF_WIN
    view = lambda t: t.reshape(b, m_len, dil * B_WIDTH)
    cur = pl.BlockSpec((1, bq, B_WIDTH), lambda bi, r, i: (bi, i, r))
    prev = pl.BlockSpec((1, HALF_WIN, B_WIDTH), lambda bi, r, i: (bi, jnp.maximum(i * sub - 1, 0), r))
    nxt = pl.BlockSpec((1, HALF_WIN, B_WIDTH), lambda bi, r, i: (bi, jnp.minimum((i + 1) * sub, n_half - 1), r))
    kern = functools.partial(_dilated_kernel, bq=bq, nblk=nblk)
    o, lse = pl.pallas_call(
        kern,
        grid=(b, dil, nblk),
        in_specs=[cur, prev, cur, nxt, prev, cur, nxt],
        out_specs=[cur, pl.BlockSpec((1, bq, LANES), lambda bi, r, i: (bi, i, r))],
        out_shape=[jax.ShapeDtypeStruct((b, m_len, dil * B_WIDTH), F32),
                   jax.ShapeDtypeStruct((b, m_len, dil * LANES), F32)],
        compiler_params=_cparams(("parallel", "parallel", "parallel")),
        name=f"dilated_d{dil}",
    )(view(qv), view(kv), view(kv), view(kv), view(vv), view(vv), view(vv))
    return o.reshape(b * m_len, dil * B_WIDTH), lse.reshape(b * m_len, dil * LANES)


def _out_proj_kernel(oa_ref, o1_ref, o2_ref, o3_ref, l1_ref, l2_ref, l3_ref, ga_ref, gb_ref, x_ref,
                     w_ref, goa_ref, gob_ref, lng_ref, lnb_ref, y_ref, on_ref, ln_ref):
    za = (_rms(oa_ref[...], goa_ref[...]) * ga_ref[...].astype(F32)).astype(BF16)
    tm = oa_ref.shape[0]
    for i, (o_ref, l_ref, (_, dil)) in enumerate(zip((o1_ref, o2_ref, o3_ref), (l1_ref, l2_ref, l3_ref),
                                                     DILATED_PATTERNS)):
        for r in range(dil):
            rows = pl.ds(r, tm // dil, stride=dil) if dil > 1 else slice(None)
            ln_ref[i, rows, :] = l_ref[:, r * LANES:(r + 1) * LANES]
            for h in range(B_HEADS):
                col = r * B_WIDTH + h * B_HDIM
                on_ref[i, h, rows, :] = o_ref[:, col:col + B_HDIM]
    l1, l2, l3 = ln_ref[0], ln_ref[1], ln_ref[2]
    mx = jnp.maximum(jnp.maximum(l1, l2), l3)
    e1, e2, e3 = jnp.exp(l1 - mx), jnp.exp(l2 - mx), jnp.exp(l3 - mx)
    inv = 1.0 / (e1 + e2 + e3)
    w1, w2, w3 = e1 * inv, e2 * inv, e3 * inv
    parts = []
    for h in range(B_HEADS):
        parts.append(w1[:, h:h + 1] * on_ref[0, h] + w2[:, h:h + 1] * on_ref[1, h] + w3[:, h:h + 1] * on_ref[2, h])
    ob = jnp.concatenate(parts, axis=1)
    zb = (_rms(ob, gob_ref[...]) * gb_ref[...].astype(F32)).astype(BF16)
    out = (jnp.dot(za, w_ref[:A_WIDTH, :], preferred_element_type=F32)
           + jnp.dot(zb, w_ref[A_WIDTH:, :], preferred_element_type=F32))
    r = DEEPNORM_ALPHA * x_ref[...] + out
    mu = jnp.mean(r, axis=-1, keepdims=True)
    d = r - mu
    var = jnp.mean(d * d, axis=-1, keepdims=True)
    y_ref[...] = d * lax.rsqrt(var + LN_EPS) * lng_ref[...] + lnb_ref[...]


def _out_proj(oa, obs, lses, ga, gb, x2, w_out, goa, gob, lng, lnb, tm):
    n = x2.shape[0]
    row = lambda w: pl.BlockSpec((tm, w), lambda i: (i, 0))
    view = lambda w: [pl.BlockSpec((tm // d, d * w), lambda i: (i, 0)) for _, d in DILATED_PATTERNS]
    full = lambda *shape: pl.BlockSpec(shape, lambda i: (0,) * len(shape))
    n_pat = len(DILATED_PATTERNS)
    return pl.pallas_call(
        _out_proj_kernel,
        grid=(n // tm,),
        in_specs=[row(A_WIDTH)] + view(B_WIDTH) + view(LANES) + [row(A_WIDTH), row(B_WIDTH), row(D_MODEL),
                  full(A_WIDTH + B_WIDTH, D_MODEL), full(1, A_WIDTH), full(1, B_WIDTH),
                  full(1, D_MODEL), full(1, D_MODEL)],
        out_specs=row(D_MODEL),
        out_shape=jax.ShapeDtypeStruct((n, D_MODEL), F32),
        scratch_shapes=[pltpu.VMEM((n_pat, B_HEADS, tm, B_HDIM), F32), pltpu.VMEM((n_pat, tm, LANES), F32)],
        compiler_params=_cparams(("parallel",)),
        name="out_proj",
    )(oa, *obs, *lses, ga, gb, x2, w_out, goa, gob, lng, lnb)


def _rope_tables(s):
    pos = jnp.arange(s, dtype=F32)

    def cos_sin(d):
        half = d // 2
        inv = ROPE_THETA ** (-jnp.arange(half, dtype=F32) * 2.0 / d)
        ang = pos[:, None] * inv[None, :]
        return jnp.cos(ang), jnp.sin(ang)

    cb, sb = cos_sin(B_HDIM)
    cos_b = jnp.concatenate([cb, cb], axis=1)
    sin_b = jnp.concatenate([-sb, sb], axis=1)
    ca, sa = cos_sin(A_ROPE)
    zeros = jnp.zeros((s, LANES - A_ROPE), F32)
    cos_k = jnp.concatenate([ca, ca, zeros], axis=1)
    sin_k = jnp.concatenate([-sa, sa, zeros], axis=1)
    return cos_b, sin_b, cos_k, sin_k, ca.T, sa.T


def _prep_weights(w_in, g_qn, w_uq, g_kvn, w_ukv, g_oa, g_ob, w_out, ln_g, ln_b):
    n_lat = Q_LORA + KV_LORA + A_ROPE
    w_seg = jnp.concatenate([w_in[:, :n_lat], jnp.zeros((D_MODEL, LAT_PAD), w_in.dtype), w_in[:, n_lat:]],
                            axis=1).astype(BF16)
    wuqt = w_uq.reshape(Q_LORA, A_HEADS, A_QK).transpose(1, 2, 0).astype(BF16)
    wukv = w_ukv.reshape(KV_LORA, A_HEADS, A_NOPE + A_VDIM)
    wuk = wukv[:, :, :A_NOPE].transpose(1, 0, 2).astype(BF16)
    wuvt = wukv[:, :, A_NOPE:].transpose(1, 2, 0).astype(BF16)
    return dict(w_seg=w_seg, gq=g_qn[None, :], gkv=g_kvn[None, :], wuqt=wuqt, wuk=wuk, wuvt=wuvt,
                goa=g_oa[None, :], gob=g_ob[None, :], w_out=w_out.astype(BF16),
                lng=ln_g[None, :], lnb=ln_b[None, :])


def _tiles(s):
    pick = lambda pref, n: pref if n % pref == 0 else n
    return dict(tm_in=pick(512, s), tm_up=pick(512, s), bq=pick(512, s), bk=pick(512, s), tm_out=pick(256, s))


def _encoder_layer(x, p):
    b, s, _ = x.shape
    longest = max(w // 2 for w, _ in DILATED_PATTERNS)
    assert s % longest == 0, "sequence length must be a multiple of the largest dilated chunk"
    t = _tiles(s)
    cos_b, sin_b, cos_k, sin_k, cos_t, sin_t = _rope_tables(s)
    x2 = x.reshape(b * s, D_MODEL)
    tm = t["tm_in"]
    w_seg = p["w_seg"]
    lat = _in_proj(x2, w_seg, 0, "lat", tm)
    qvs = _in_proj(x2, w_seg, 1, "rope", tm, s, cos_b, sin_b, B_SCALE)
    kvs = _in_proj(x2, w_seg, 2, "rope", tm, s, cos_b, sin_b)
    vvs = _in_proj(x2, w_seg, 3, "plain", tm)
    ga = _in_proj(x2, w_seg, 4, "silu", tm)
    gb = _in_proj(x2, w_seg, 5, "silu", tm)
    qt, k, vt = _mla_up(lat.reshape(b, s, SEG), p["gq"], p["gkv"], p["wuqt"], p["wuk"], p["wuvt"],
                        cos_k, sin_k, cos_t, sin_t, t["tm_up"])
    oa = _mla_flash(qt, k, vt, t["bq"], t["bk"]).reshape(b * s, A_WIDTH)
    obs, lses = [], []
    for i, (window, dil) in enumerate(DILATED_PATTERNS):
        assert window // (2 * dil) == HALF_WIN
        o, lse = _dilated(qvs[i], kvs[i], vvs[i], b, dil, min(256, s // dil))
        obs.append(o)
        lses.append(lse)
    y = _out_proj(oa, obs, lses, ga, gb, x2, p["w_out"], p["goa"], p["gob"], p["lng"], p["lnb"], t["tm_out"])
    return y.reshape(b, s, D_MODEL)


def kernel(x_prompt, x_sample, w_in, g_qn, w_uq, g_kvn, w_ukv, g_oa, g_ob, w_out, ln_g, ln_b):
    def trunk(x):
        for l in range(DEPTH):
            p = _prep_weights(w_in[l], g_qn[l], w_uq[l], g_kvn[l], w_ukv[l], g_oa[l], g_ob[l],
                              w_out[l], ln_g[l], ln_b[l])
            x = _encoder_layer(x, p)
        return x

    return (trunk(x_prompt), trunk(x_sample))
```

```python
import functools

import jax
import jax.numpy as jnp
from jax import lax
from jax.experimental import pallas as pl
from jax.experimental.pallas import tpu as pltpu

F32 = jnp.float32
BF16 = jnp.bfloat16

D_MODEL = 2048
DEPTH = 1
A_HEADS = 8
A_NOPE = 128
A_ROPE = 64
A_VDIM = 128
A_QK = A_NOPE + A_ROPE
A_WIDTH = A_HEADS * A_VDIM
A_VROWS = A_VDIM + 16
Q_LORA = 512
KV_LORA = 256
B_HEADS = 8
B_HDIM = 128
B_WIDTH = B_HEADS * B_HDIM
DILATED_PATTERNS = ((128, 1), (512, 4), (2048, 16))
ROPE_THETA = 10000.0
RMS_EPS = 1e-6
LN_EPS = 1e-5
NEG_BIG = -1e30
DEEPNORM_ALPHA = (2 * DEPTH) ** 0.25
LOG2E = 1.4426950408889634
A_SCALE = A_QK ** -0.5 * LOG2E
B_SCALE = B_HDIM ** -0.5

LANES = 128
SEG = 1024
N_SEG = 6
LAT_PAD = SEG - (Q_LORA + KV_LORA + A_ROPE)
HALF_WIN = 64
VMEM_LIMIT = 56 * 1024 * 1024


def _cparams(semantics, flags=None):
    return pltpu.CompilerParams(dimension_semantics=semantics, vmem_limit_bytes=VMEM_LIMIT, flags=flags)


def _nt_dot(a, b):
    return lax.dot_general(a, b, (((1,), (1,)), ((), ())), preferred_element_type=F32)


def _in_proj_kernel(x_ref, w_ref, *refs, mode, scale):
    acc = jnp.dot(x_ref[...].astype(BF16), w_ref[...], preferred_element_type=F32)
    if mode == "lat":
        refs[0][...] = acc
        return
    if mode == "silu":
        refs[0][...] = (acc * jax.nn.sigmoid(acc)).astype(BF16)
        return
    if mode == "rope":
        cos_ref, sin_ref = refs[:2]
        refs = refs[2:]
    outs, seg_ref = refs[:-1], refs[-1]
    if mode == "rope":
        cos = cos_ref[...]
        sin = sin_ref[...]
    tm = seg_ref.shape[1]
    for h in range(B_HEADS):
        xh = acc[:, h * B_HDIM:(h + 1) * B_HDIM]
        if mode == "rope":
            xh = (xh * cos + pltpu.roll(xh, B_HDIM // 2, 1) * sin) * scale
        seg_ref[h] = xh
        for o_ref, (_, dil) in zip(outs, DILATED_PATTERNS):
            for r in range(dil):
                rows = pl.ds(r, tm // dil, stride=dil) if dil > 1 else slice(None)
                col = r * SEG + h * B_HDIM
                o_ref[:, col:col + B_HDIM] = seg_ref[h, rows, :].astype(BF16)


def _in_proj(x2, w_seg, seg, mode, tm, s=None, cos_b=None, sin_b=None, scale=1.0):
    n = x2.shape[0]
    row = lambda i: (i, 0)
    in_specs = [pl.BlockSpec((tm, D_MODEL), row), pl.BlockSpec((D_MODEL, SEG), lambda i: (0, seg))]
    args = [x2, w_seg]
    scratch = []
    if mode == "rope":
        n_pos = s // tm
        in_specs += [pl.BlockSpec((tm, B_HDIM), lambda i: (i % n_pos, 0))] * 2
        args += [cos_b, sin_b]
    if mode in ("lat", "silu"):
        out_specs = pl.BlockSpec((tm, SEG), row)
        out_shape = jax.ShapeDtypeStruct((n, SEG), F32 if mode == "lat" else BF16)
    else:
        out_specs = [pl.BlockSpec((tm // d, d * SEG), row) for _, d in DILATED_PATTERNS]
        out_shape = [jax.ShapeDtypeStruct((n // d, d * SEG), BF16) for _, d in DILATED_PATTERNS]
        scratch = [pltpu.VMEM((B_HEADS, tm, B_HDIM), F32)]
    return pl.pallas_call(
        functools.partial(_in_proj_kernel, mode=mode, scale=scale),
        grid=(n // tm,),
        in_specs=in_specs,
        out_specs=out_specs,
        out_shape=out_shape,
        scratch_shapes=scratch,
        compiler_params=_cparams(("parallel",)),
        name=f"in_proj_{seg}_{mode}",
    )(*args)


def _rms(x, g):
    return x * lax.rsqrt(jnp.mean(x * x, axis=-1, keepdims=True) + RMS_EPS) * g


def _mla_up_kernel(lat_ref, gq_ref, gkv_ref, wuqt_ref, wuk_ref, wuvt_ref,
                   cosk_ref, sink_ref, cost_ref, sint_ref, qt_ref, k_ref, vt_ref):
    lat = lat_ref[0]
    cqn = _rms(lat[:, :Q_LORA], gq_ref[...]).astype(BF16)
    ckvn = _rms(lat[:, Q_LORA:Q_LORA + KV_LORA], gkv_ref[...]).astype(BF16)
    kr = lat[:, Q_LORA + KV_LORA:Q_LORA + KV_LORA + LANES]
    lane = lax.broadcasted_iota(jnp.int32, kr.shape, 1)
    half = A_ROPE // 2
    rot = jnp.where(lane < half, pltpu.roll(kr, LANES - half, 1), pltpu.roll(kr, half, 1))
    kpe = (kr * cosk_ref[...] + rot * sink_ref[...]).astype(BF16)
    ct = cost_ref[...]
    st = sint_ref[...]
    for h in range(A_HEADS):
        qt = _nt_dot(wuqt_ref[h], cqn)
        x1 = qt[A_NOPE:A_NOPE + half]
        x2 = qt[A_NOPE + half:]
        qt_ref[0, h, 0:A_NOPE, :] = (qt[:A_NOPE] * A_SCALE).astype(BF16)
        qt_ref[0, h, A_NOPE:A_NOPE + half, :] = ((x1 * ct - x2 * st) * A_SCALE).astype(BF16)
        qt_ref[0, h, A_NOPE + half:, :] = ((x2 * ct + x1 * st) * A_SCALE).astype(BF16)
        k_ref[0, h, :, 0:A_NOPE] = jnp.dot(ckvn, wuk_ref[h], preferred_element_type=F32).astype(BF16)
        k_ref[0, h, :, A_NOPE:] = kpe[:, :A_ROPE]
        vt_ref[0, h, 0:A_VDIM, :] = _nt_dot(wuvt_ref[h], ckvn).astype(BF16)
        vt_ref[0, h, A_VDIM:, :] = jnp.ones((A_VROWS - A_VDIM, vt_ref.shape[3]), BF16)


def _mla_up(lat3, gq, gkv, wuqt, wuk, wuvt, cosk, sink, cost, sint, tm):
    b, s, _ = lat3.shape
    full = lambda *shape: pl.BlockSpec(shape, lambda bi, si: (0,) * len(shape))
    return pl.pallas_call(
        _mla_up_kernel,
        grid=(b, s // tm),
        in_specs=[
            pl.BlockSpec((1, tm, SEG), lambda bi, si: (bi, si, 0)),
            full(1, Q_LORA), full(1, KV_LORA),
            full(A_HEADS, A_QK, Q_LORA), full(A_HEADS, KV_LORA, A_NOPE), full(A_HEADS, A_VDIM, KV_LORA),
            pl.BlockSpec((tm, LANES), lambda bi, si: (si, 0)),
            pl.BlockSpec((tm, LANES), lambda bi, si: (si, 0)),
            pl.BlockSpec((A_ROPE // 2, tm), lambda bi, si: (0, si)),
            pl.BlockSpec((A_ROPE // 2, tm), lambda bi, si: (0, si)),
        ],
        out_specs=[
            pl.BlockSpec((1, A_HEADS, A_QK, tm), lambda bi, si: (bi, 0, 0, si)),
            pl.BlockSpec((1, A_HEADS, tm, A_QK), lambda bi, si: (bi, 0, si, 0)),
            pl.BlockSpec((1, A_HEADS, A_VROWS, tm), lambda bi, si: (bi, 0, 0, si)),
        ],
        out_shape=[
            jax.ShapeDtypeStruct((b, A_HEADS, A_QK, s), BF16),
            jax.ShapeDtypeStruct((b, A_HEADS, s, A_QK), BF16),
            jax.ShapeDtypeStruct((b, A_HEADS, A_VROWS, s), BF16),
        ],
        compiler_params=_cparams(("parallel", "parallel")),
        name="mla_up",
    )(lat3, gq, gkv, wuqt, wuk, wuvt, cosk, sink, cost, sint)


FLASH_BUFS = 3
FLASH_FLAGS = None


def _mla_flash_kernel(qt_ref, k_ref, vt_ref, o_ref, s_ref, p_ref, acc_ref, *, bk, nk):
    qt = qt_ref[0, 0]
    bq = qt.shape[1]

    def scores(t, slot):
        start = pl.multiple_of(t * bk, bk)
        st = jnp.dot(k_ref[0, 0, pl.ds(start, bk), :], qt, preferred_element_type=F32)
        s_ref[slot] = st
        return jnp.max(st, axis=0, keepdims=True)

    def softmax(slot, m, cmax):
        m_new = jnp.maximum(m, cmax)
        p_ref[slot] = jnp.exp2((s_ref[slot] - m_new).astype(BF16))
        return m_new, jnp.exp2(m - m_new)

    def values(t, slot, alpha):
        start = pl.multiple_of(t * bk, bk)
        vt = vt_ref[0, 0, :, pl.ds(start, bk)]
        acc_ref[...] = alpha * acc_ref[...] + jnp.dot(vt, p_ref[slot], preferred_element_type=F32)

    def step(t, t_slot, carry, last=False):
        m, alpha, cmax = carry
        cmax_next = cmax if last else scores(t + 1, (t_slot + 1) % FLASH_BUFS)
        m, alpha_t = softmax(t_slot, m, cmax)
        values(t - 1, (t_slot - 1) % FLASH_BUFS, alpha)
        return m, alpha_t, cmax_next

    acc_ref[...] = jnp.zeros(acc_ref.shape, F32)
    cmax0 = scores(0, 0)
    cmax1 = scores(1, 1)
    carry = softmax(0, jnp.full((1, bq), NEG_BIG, F32), cmax0) + (cmax1,)

    def group(j, carry):
        for c in range(FLASH_BUFS):
            carry = step(FLASH_BUFS * j + 1 + c, (1 + c) % FLASH_BUFS, carry)
        return carry

    n_groups = (nk - 2) // FLASH_BUFS
    carry = lax.fori_loop(0, n_groups, group, carry)
    for t in range(FLASH_BUFS * n_groups + 1, nk):
        carry = step(t, t % FLASH_BUFS, carry, last=(t == nk - 1))
    values(nk - 1, (nk - 1) % FLASH_BUFS, carry[1])
    acc = acc_ref[...]
    o = acc[:A_VDIM] * (1.0 / acc[A_VDIM:A_VDIM + 1])
    o_ref[0] = o.T


def _mla_flash(qt, k, vt, bq, bk):
    b, h, _, s = qt.shape
    nk = s // bk
    assert nk >= 2
    kern = functools.partial(_mla_flash_kernel, bk=bk, nk=nk)
    return pl.pallas_call(
        kern,
        grid=(b, h, s // bq),
        in_specs=[
            pl.BlockSpec((1, 1, A_QK, bq), lambda bi, hi, qi: (bi, hi, 0, qi)),
            pl.BlockSpec((1, 1, s, A_QK), lambda bi, hi, qi: (bi, hi, 0, 0)),
            pl.BlockSpec((1, 1, A_VROWS, s), lambda bi, hi, qi: (bi, hi, 0, 0)),
        ],
        out_specs=pl.BlockSpec((1, bq, A_VDIM), lambda bi, hi, qi: (bi, qi, hi)),
        out_shape=jax.ShapeDtypeStruct((b, s, A_WIDTH), F32),
        scratch_shapes=[pltpu.VMEM((FLASH_BUFS, bk, bq), F32), pltpu.VMEM((FLASH_BUFS, bk, bq), BF16),
                        pltpu.VMEM((A_VROWS, bq), F32)],
        compiler_params=_cparams(("parallel", "parallel", "arbitrary"), FLASH_FLAGS),
        name="mla_flash",
    )(qt, k, vt)


def _dilated_kernel(q_ref, kp_ref, kc_ref, kn_ref, vp_ref, vc_ref, vn_ref, o_ref, lse_ref, *, bq, nblk):
    i = pl.program_id(2)
    nkeys = bq + 2 * HALF_WIN
    a = lax.broadcasted_iota(jnp.int32, (bq, nkeys), 0)
    c = lax.broadcasted_iota(jnp.int32, (bq, nkeys), 1)
    band = (c >= a) & (c <= a + 2 * HALF_WIN)
    valid = ((c >= HALF_WIN) | (i > 0)) & ((c < bq + HALF_WIN) | (i < nblk - 1))
    mask = band & valid
    lane = lax.broadcasted_iota(jnp.int32, (bq, LANES), 1)
    lse_all = jnp.zeros((bq, LANES), F32)
    for h in range(B_HEADS):
        hs = slice(h * B_HDIM, (h + 1) * B_HDIM)
        q = q_ref[0, :, hs]
        k = jnp.concatenate([kp_ref[0, :, hs], kc_ref[0, :, hs], kn_ref[0, :, hs]], axis=0)
        v = jnp.concatenate([vp_ref[0, :, hs], vc_ref[0, :, hs], vn_ref[0, :, hs]], axis=0)
        sc = jnp.where(mask, _nt_dot(q, k), NEG_BIG)
        m = jnp.max(sc, axis=1, keepdims=True)
        p = jnp.exp(sc - m)
        den = jnp.sum(p, axis=1, keepdims=True)
        num = jnp.dot(p.astype(BF16), v, preferred_element_type=F32)
        o_ref[0, :, hs] = num * (1.0 / den)
        lse_all = jnp.where(lane == h, m + jnp.log(den), lse_all)
    lse_ref[0] = lse_all


def _dilated(qv, kv, vv, b, dil, bq):
    m_len = qv.shape[0] // b
    nblk = m_len // bq
    sub = bq // HALF_WIN
    n_half = m_len // HALF_WIN
    view = lambda t: t.reshape(b, m_len, dil * B_WIDTH)
    cur = pl.BlockSpec((1, bq, B_WIDTH), lambda bi, r, i: (bi, i, r))
    prev = pl.BlockSpec((1, HALF_WIN, B_WIDTH), lambda bi, r, i: (bi, jnp.maximum(i * sub - 1, 0), r))
    nxt = pl.BlockSpec((1, HALF_WIN, B_WIDTH), lambda bi, r, i: (bi, jnp.minimum((i + 1) * sub, n_half - 1), r))
    kern = functools.partial(_dilated_kernel, bq=bq, nblk=nblk)
    o, lse = pl.pallas_call(
        kern,
        grid=(b, dil, nblk),
        in_specs=[cur, prev, cur, nxt, prev, cur, nxt],
        out_specs=[cur, pl.BlockSpec((1, bq, LANES), lambda bi, r, i: (bi, i, r))],
        out_shape=[jax.ShapeDtypeStruct((b, m_len, dil * B_WIDTH), F32),
                   jax.ShapeDtypeStruct((b, m_len, dil * LANES), F32)],
        compiler_params=_cparams(("parallel", "parallel", "parallel")),
        name=f"dilated_d{dil}",
    )(view(qv), view(kv), view(kv), view(kv), view(vv), view(vv), view(vv))
    return o.reshape(b * m_len, dil * B_WIDTH), lse.reshape(b * m_len, dil * LANES)


def _out_proj_kernel(oa_ref, o1_ref, o2_ref, o3_ref, l1_ref, l2_ref, l3_ref, ga_ref, gb_ref, x_ref,
                     w_ref, goa_ref, gob_ref, lng_ref, lnb_ref, y_ref, on_ref, ln_ref):
    za = (_rms(oa_ref[...], goa_ref[...]) * ga_ref[...].astype(F32)).astype(BF16)
    tm = oa_ref.shape[0]
    for i, (o_ref, l_ref, (_, dil)) in enumerate(zip((o1_ref, o2_ref, o3_ref), (l1_ref, l2_ref, l3_ref),
                                                     DILATED_PATTERNS)):
        for r in range(dil):
            rows = pl.ds(r, tm // dil, stride=dil) if dil > 1 else slice(None)
            ln_ref[i, rows, :] = l_ref[:, r * LANES:(r + 1) * LANES]
            for h in range(B_HEADS):
                col = r * B_WIDTH + h * B_HDIM
                on_ref[i, h, rows, :] = o_ref[:, col:col + B_HDIM]
    l1, l2, l3 = ln_ref[0], ln_ref[1], ln_ref[2]
    mx = jnp.maximum(jnp.maximum(l1, l2), l3)
    e1, e2, e3 = jnp.exp(l1 - mx), jnp.exp(l2 - mx), jnp.exp(l3 - mx)
    inv = 1.0 / (e1 + e2 + e3)
    w1, w2, w3 = e1 * inv, e2 * inv, e3 * inv
    parts = []
    for h in range(B_HEADS):
        parts.append(w1[:, h:h + 1] * on_ref[0, h] + w2[:, h:h + 1] * on_ref[1, h] + w3[:, h:h + 1] * on_ref[2, h])
    ob = jnp.concatenate(parts, axis=1)
    zb = (_rms(ob, gob_ref[...]) * gb_ref[...].astype(F32)).astype(BF16)
    out = (jnp.dot(za, w_ref[:A_WIDTH, :], preferred_element_type=F32)
           + jnp.dot(zb, w_ref[A_WIDTH:, :], preferred_element_type=F32))
    r = DEEPNORM_ALPHA * x_ref[...] + out
    mu = jnp.mean(r, axis=-1, keepdims=True)
    d = r - mu
    var = jnp.mean(d * d, axis=-1, keepdims=True)
    y_ref[...] = d * lax.rsqrt(var + LN_EPS) * lng_ref[...] + lnb_ref[...]


def _out_proj(oa, obs, lses, ga, gb, x2, w_out, goa, gob, lng, lnb, tm):
    n = x2.shape[0]
    row = lambda w: pl.BlockSpec((tm, w), lambda i: (i, 0))
    view = lambda w: [pl.BlockSpec((tm // d, d * w), lambda i: (i, 0)) for _, d in DILATED_PATTERNS]
    full = lambda *shape: pl.BlockSpec(shape, lambda i: (0,) * len(shape))
    n_pat = len(DILATED_PATTERNS)
    return pl.pallas_call(
        _out_proj_kernel,
        grid=(n // tm,),
        in_specs=[row(A_WIDTH)] + view(B_WIDTH) + view(LANES) + [row(A_WIDTH), row(B_WIDTH), row(D_MODEL),
                  full(A_WIDTH + B_WIDTH, D_MODEL), full(1, A_WIDTH), full(1, B_WIDTH),
                  full(1, D_MODEL), full(1, D_MODEL)],
        out_specs=row(D_MODEL),
        out_shape=jax.ShapeDtypeStruct((n, D_MODEL), F32),
        scratch_shapes=[pltpu.VMEM((n_pat, B_HEADS, tm, B_HDIM), F32), pltpu.VMEM((n_pat, tm, LANES), F32)],
        compiler_params=_cparams(("parallel",)),
        name="out_proj",
    )(oa, *obs, *lses, ga, gb, x2, w_out, goa, gob, lng, lnb)


def _rope_tables(s):
    pos = jnp.arange(s, dtype=F32)

    def cos_sin(d):
        half = d // 2
        inv = ROPE_THETA ** (-jnp.arange(half, dtype=F32) * 2.0 / d)
        ang = pos[:, None] * inv[None, :]
        return jnp.cos(ang), jnp.sin(ang)

    cb, sb = cos_sin(B_HDIM)
    cos_b = jnp.concatenate([cb, cb], axis=1)
    sin_b = jnp.concatenate([-sb, sb], axis=1)
    ca, sa = cos_sin(A_ROPE)
    zeros = jnp.zeros((s, LANES - A_ROPE), F32)
    cos_k = jnp.concatenate([ca, ca, zeros], axis=1)
    sin_k = jnp.concatenate([-sa, sa, zeros], axis=1)
    return cos_b, sin_b, cos_k, sin_k, ca.T, sa.T


def _prep_weights(w_in, g_qn, w_uq, g_kvn, w_ukv, g_oa, g_ob, w_out, ln_g, ln_b):
    n_lat = Q_LORA + KV_LORA + A_ROPE
    w_seg = jnp.concatenate([w_in[:, :n_lat], jnp.zeros((D_MODEL, LAT_PAD), w_in.dtype), w_in[:, n_lat:]],
                            axis=1).astype(BF16)
    wuqt = w_uq.reshape(Q_LORA, A_HEADS, A_QK).transpose(1, 2, 0).astype(BF16)
    wukv = w_ukv.reshape(KV_LORA, A_HEADS, A_NOPE + A_VDIM)
    wuk = wukv[:, :, :A_NOPE].transpose(1, 0, 2).astype(BF16)
    wuvt = wukv[:, :, A_NOPE:].transpose(1, 2, 0).astype(BF16)
    return dict(w_seg=w_seg, gq=g_qn[None, :], gkv=g_kvn[None, :], wuqt=wuqt, wuk=wuk, wuvt=wuvt,
                goa=g_oa[None, :], gob=g_ob[None, :], w_out=w_out.astype(BF16),
                lng=ln_g[None, :], lnb=ln_b[None, :])


def _tiles(s):
    pick = lambda pref, n: pref if n % pref == 0 else n
    return dict(tm_in=pick(512, s), tm_up=pick(512, s), bq=pick(512, s), bk=pick(512, s), tm_out=pick(256, s))


def _encoder_layer(x, p):
    b, s, _ = x.shape
    longest = max(w // 2 for w, _ in DILATED_PATTERNS)
    assert s % longest == 0, "sequence length must be a multiple of the largest dilated chunk"
    t = _tiles(s)
    cos_b, sin_b, cos_k, sin_k, cos_t, sin_t = _rope_tables(s)
    x2 = x.reshape(b * s, D_MODEL)
    tm = t["tm_in"]
    w_seg = p["w_seg"]
    lat = _in_proj(x2, w_seg, 0, "lat", tm)
    qvs = _in_proj(x2, w_seg, 1, "rope", tm, s, cos_b, sin_b, B_SCALE)
    kvs = _in_proj(x2, w_seg, 2, "rope", tm, s, cos_b, sin_b)
    vvs = _in_proj(x2, w_seg, 3, "plain", tm)
    ga = _in_proj(x2, w_seg, 4, "silu", tm)
    gb = _in_proj(x2, w_seg, 5, "silu", tm)
    qt, k, vt = _mla_up(lat.reshape(b, s, SEG), p["gq"], p["gkv"], p["wuqt"], p["wuk"], p["wuvt"],
                        cos_k, sin_k, cos_t, sin_t, t["tm_up"])
    oa = _mla_flash(qt, k, vt, t["bq"], t["bk"]).reshape(b * s, A_WIDTH)
    obs, lses = [], []
    for i, (window, dil) in enumerate(DILATED_PATTERNS):
        assert window // (2 * dil) == HALF_WIN
        o, lse = _dilated(qvs[i], kvs[i], vvs[i], b, dil, min(256, s // dil))
        obs.append(o)
        lses.append(lse)
    y = _out_proj(oa, obs, lses, ga, gb, x2, p["w_out"], p["goa"], p["gob"], p["lng"], p["lnb"], t["tm_out"])
    return y.reshape(b, s, D_MODEL)


def kernel(x_prompt, x_sample, w_in, g_qn, w_uq, g_kvn, w_ukv, g_oa, g_ob, w_out, ln_g, ln_b):
    def trunk(x):
        for l in range(DEPTH):
            p = _prep_weights(w_in[l], g_qn[l], w_uq[l], g_kvn[l], w_ukv[l], g_oa[l], g_ob[l],
                              w_out[l], ln_g[l], ln_b[l])
            x = _encoder_layer(x, p)
        return x

    return (trunk(x_prompt), trunk(x_sample))
```

```python
import functools

import jax
import jax.numpy as jnp
from jax import lax
from jax.experimental import pallas as pl
from jax.experimental.pallas import tpu as pltpu

F32 = jnp.float32
BF16 = jnp.bfloat16

D_MODEL = 2048
DEPTH = 1
A_HEADS = 8
A_NOPE = 128
A_ROPE = 64
A_VDIM = 128
A_QK = A_NOPE + A_ROPE
A_WIDTH = A_HEADS * A_VDIM
A_VROWS = A_VDIM + 16
Q_LORA = 512
KV_LORA = 256
B_HEADS = 8
B_HDIM = 128
B_WIDTH = B_HEADS * B_HDIM
DILATED_PATTERNS = ((128, 1), (512, 4), (2048, 16))
ROPE_THETA = 10000.0
RMS_EPS = 1e-6
LN_EPS = 1e-5
NEG_BIG = -1e30
DEEPNORM_ALPHA = (2 * DEPTH) ** 0.25
LOG2E = 1.4426950408889634
A_SCALE = A_QK ** -0.5 * LOG2E
B_SCALE = B_HDIM ** -0.5 * LOG2E

LANES = 128
SEG = 1024
N_SEG = 6
LAT_PAD = SEG - (Q_LORA + KV_LORA + A_ROPE)
HALF_WIN = 64
DIL_SUBQ = 128
DIL_BQ = 512
VMEM_LIMIT = 56 * 1024 * 1024


def _cparams(semantics, flags=None):
    return pltpu.CompilerParams(dimension_semantics=semantics, vmem_limit_bytes=VMEM_LIMIT, flags=flags)


def _nt_dot(a, b):
    return lax.dot_general(a, b, (((1,), (1,)), ((), ())), preferred_element_type=F32)


def _in_proj_kernel(x_ref, w_ref, *refs, mode, scale):
    acc = jnp.dot(x_ref[...].astype(BF16), w_ref[...], preferred_element_type=F32)
    if mode == "lat":
        refs[0][...] = acc
        return
    if mode == "silu":
        refs[0][...] = (acc * jax.nn.sigmoid(acc)).astype(BF16)
        return
    if mode == "rope":
        cos_ref, sin_ref = refs[:2]
        refs = refs[2:]
    outs, seg_ref = refs[:-1], refs[-1]
    if mode == "rope":
        cos = cos_ref[...]
        sin = sin_ref[...]
    tm = seg_ref.shape[1]
    for h in range(B_HEADS):
        xh = acc[:, h * B_HDIM:(h + 1) * B_HDIM]
        if mode == "rope":
            xh = (xh * cos + pltpu.roll(xh, B_HDIM // 2, 1) * sin) * scale
        seg_ref[h] = xh
        for o_ref, (_, dil) in zip(outs, DILATED_PATTERNS):
            for r in range(dil):
                rows = pl.ds(r, tm // dil, stride=dil) if dil > 1 else slice(None)
                col = r * SEG + h * B_HDIM
                o_ref[:, col:col + B_HDIM] = seg_ref[h, rows, :].astype(BF16)


def _in_proj(x2, w_seg, seg, mode, tm, s=None, cos_b=None, sin_b=None, scale=1.0):
    n = x2.shape[0]
    row = lambda i: (i, 0)
    in_specs = [pl.BlockSpec((tm, D_MODEL), row), pl.BlockSpec((D_MODEL, SEG), lambda i: (0, seg))]
    args = [x2, w_seg]
    scratch = []
    if mode == "rope":
        n_pos = s // tm
        in_specs += [pl.BlockSpec((tm, B_HDIM), lambda i: (i % n_pos, 0))] * 2
        args += [cos_b, sin_b]
    if mode in ("lat", "silu"):
        out_specs = pl.BlockSpec((tm, SEG), row)
        out_shape = jax.ShapeDtypeStruct((n, SEG), F32 if mode == "lat" else BF16)
    else:
        out_specs = [pl.BlockSpec((tm // d, d * SEG), row) for _, d in DILATED_PATTERNS]
        out_shape = [jax.ShapeDtypeStruct((n // d, d * SEG), BF16) for _, d in DILATED_PATTERNS]
        scratch = [pltpu.VMEM((B_HEADS, tm, B_HDIM), F32)]
    return pl.pallas_call(
        functools.partial(_in_proj_kernel, mode=mode, scale=scale),
        grid=(n // tm,),
        in_specs=in_specs,
        out_specs=out_specs,
        out_shape=out_shape,
        scratch_shapes=scratch,
        compiler_params=_cparams(("parallel",)),
        name=f"in_proj_{seg}_{mode}",
    )(*args)


def _rms(x, g):
    return x * lax.rsqrt(jnp.mean(x * x, axis=-1, keepdims=True) + RMS_EPS) * g


def _mla_up_kernel(lat_ref, gq_ref, gkv_ref, wuqt_ref, wuk_ref, wuvt_ref,
                   cosk_ref, sink_ref, cost_ref, sint_ref, qt_ref, k_ref, vt_ref):
    lat = lat_ref[0]
    cqn = _rms(lat[:, :Q_LORA], gq_ref[...]).astype(BF16)
    ckvn = _rms(lat[:, Q_LORA:Q_LORA + KV_LORA], gkv_ref[...]).astype(BF16)
    kr = lat[:, Q_LORA + KV_LORA:Q_LORA + KV_LORA + LANES]
    lane = lax.broadcasted_iota(jnp.int32, kr.shape, 1)
    half = A_ROPE // 2
    rot = jnp.where(lane < half, pltpu.roll(kr, LANES - half, 1), pltpu.roll(kr, half, 1))
    kpe = (kr * cosk_ref[...] + rot * sink_ref[...]).astype(BF16)
    ct = cost_ref[...]
    st = sint_ref[...]
    for h in range(A_HEADS):
        qt = _nt_dot(wuqt_ref[h], cqn)
        x1 = qt[A_NOPE:A_NOPE + half]
        x2 = qt[A_NOPE + half:]
        qt_ref[0, h, 0:A_NOPE, :] = (qt[:A_NOPE] * A_SCALE).astype(BF16)
        qt_ref[0, h, A_NOPE:A_NOPE + half, :] = ((x1 * ct - x2 * st) * A_SCALE).astype(BF16)
        qt_ref[0, h, A_NOPE + half:, :] = ((x2 * ct + x1 * st) * A_SCALE).astype(BF16)
        k_ref[0, h, :, 0:A_NOPE] = jnp.dot(ckvn, wuk_ref[h], preferred_element_type=F32).astype(BF16)
        k_ref[0, h, :, A_NOPE:] = kpe[:, :A_ROPE]
        vt_ref[0, h, 0:A_VDIM, :] = _nt_dot(wuvt_ref[h], ckvn).astype(BF16)
        vt_ref[0, h, A_VDIM:, :] = jnp.ones((A_VROWS - A_VDIM, vt_ref.shape[3]), BF16)


def _mla_up(lat3, gq, gkv, wuqt, wuk, wuvt, cosk, sink, cost, sint, tm):
    b, s, _ = lat3.shape
    full = lambda *shape: pl.BlockSpec(shape, lambda bi, si: (0,) * len(shape))
    return pl.pallas_call(
        _mla_up_kernel,
        grid=(b, s // tm),
        in_specs=[
            pl.BlockSpec((1, tm, SEG), lambda bi, si: (bi, si, 0)),
            full(1, Q_LORA), full(1, KV_LORA),
            full(A_HEADS, A_QK, Q_LORA), full(A_HEADS, KV_LORA, A_NOPE), full(A_HEADS, A_VDIM, KV_LORA),
            pl.BlockSpec((tm, LANES), lambda bi, si: (si, 0)),
            pl.BlockSpec((tm, LANES), lambda bi, si: (si, 0)),
            pl.BlockSpec((A_ROPE // 2, tm), lambda bi, si: (0, si)),
            pl.BlockSpec((A_ROPE // 2, tm), lambda bi, si: (0, si)),
        ],
        out_specs=[
            pl.BlockSpec((1, A_HEADS, A_QK, tm), lambda bi, si: (bi, 0, 0, si)),
            pl.BlockSpec((1, A_HEADS, tm, A_QK), lambda bi, si: (bi, 0, si, 0)),
            pl.BlockSpec((1, A_HEADS, A_VROWS, tm), lambda bi, si: (bi, 0, 0, si)),
        ],
        out_shape=[
            jax.ShapeDtypeStruct((b, A_HEADS, A_QK, s), BF16),
            jax.ShapeDtypeStruct((b, A_HEADS, s, A_QK), BF16),
            jax.ShapeDtypeStruct((b, A_HEADS, A_VROWS, s), BF16),
        ],
        compiler_params=_cparams(("parallel", "parallel")),
        name="mla_up",
    )(lat3, gq, gkv, wuqt, wuk, wuvt, cosk, sink, cost, sint)


FLASH_BUFS = 3
FLASH_UNROLL = 6
FLASH_FLAGS = None


def _mla_flash_kernel(qt_ref, k_ref, vt_ref, o_ref, s_ref, p_ref, acc_ref, *, bk, nk):
    bq = qt_ref.shape[3]

    def scores(t, slot):
        start = pl.multiple_of(t * bk, bk)
        st = jnp.dot(k_ref[0, 0, pl.ds(start, bk), :], qt_ref[0, 0], preferred_element_type=F32)
        s_ref[slot] = st
        return jnp.max(st, axis=0, keepdims=True)

    def softmax(slot, m, cmax):
        m_new = jnp.maximum(m, cmax)
        p_ref[slot] = jnp.exp2((s_ref[slot] - m_new).astype(BF16))
        return m_new, jnp.exp2(m - m_new)

    def values(t, slot, alpha):
        start = pl.multiple_of(t * bk, bk)
        vt = vt_ref[0, 0, :, pl.ds(start, bk)]
        acc_ref[...] = alpha * acc_ref[...] + jnp.dot(vt, p_ref[slot], preferred_element_type=F32)

    def step(t, t_slot, carry, last=False):
        m, alpha, cmax = carry
        cmax_next = cmax if last else scores(t + 1, (t_slot + 1) % FLASH_BUFS)
        m, alpha_t = softmax(t_slot, m, cmax)
        values(t - 1, (t_slot - 1) % FLASH_BUFS, alpha)
        return m, alpha_t, cmax_next

    acc_ref[...] = jnp.zeros(acc_ref.shape, F32)
    cmax0 = scores(0, 0)
    cmax1 = scores(1, 1)
    carry = softmax(0, jnp.full((1, bq), NEG_BIG, F32), cmax0) + (cmax1,)

    def group(j, carry):
        for c in range(FLASH_UNROLL):
            carry = step(FLASH_UNROLL * j + 1 + c, (1 + c) % FLASH_BUFS, carry)
        return carry

    n_groups = (nk - 2) // FLASH_UNROLL
    carry = lax.fori_loop(0, n_groups, group, carry)
    for t in range(FLASH_UNROLL * n_groups + 1, nk):
        carry = step(t, t % FLASH_BUFS, carry, last=(t == nk - 1))
    values(nk - 1, (nk - 1) % FLASH_BUFS, carry[1])
    acc = acc_ref[...]
    o = acc[:A_VDIM] * (1.0 / acc[A_VDIM:A_VDIM + 1])
    o_ref[0] = o.T


def _mla_flash(qt, k, vt, bq, bk):
    b, h, _, s = qt.shape
    nk = s // bk
    assert nk >= 2
    kern = functools.partial(_mla_flash_kernel, bk=bk, nk=nk)
    return pl.pallas_call(
        kern,
        grid=(b, h, s // bq),
        in_specs=[
            pl.BlockSpec((1, 1, A_QK, bq), lambda bi, hi, qi: (bi, hi, 0, qi)),
            pl.BlockSpec((1, 1, s, A_QK), lambda bi, hi, qi: (bi, hi, 0, 0)),
            pl.BlockSpec((1, 1, A_VROWS, s), lambda bi, hi, qi: (bi, hi, 0, 0)),
        ],
        out_specs=pl.BlockSpec((1, bq, A_VDIM), lambda bi, hi, qi: (bi, qi, hi)),
        out_shape=jax.ShapeDtypeStruct((b, s, A_WIDTH), F32),
        scratch_shapes=[pltpu.VMEM((FLASH_BUFS, bk, bq), F32), pltpu.VMEM((FLASH_BUFS, bk, bq), BF16),
                        pltpu.VMEM((A_VROWS, bq), F32)],
        compiler_params=_cparams(("parallel", "parallel", "arbitrary"), FLASH_FLAGS),
        name="mla_flash",
    )(qt, k, vt)


def _dilated_kernel(q_ref, kp_ref, kc_ref, kn_ref, vp_ref, vc_ref, vn_ref, o_ref, lse_ref, *, bq, nblk):
    i = pl.program_id(2)
    nkeys = DIL_SUBQ + 2 * HALF_WIN
    n_sub = bq // DIL_SUBQ
    a = lax.broadcasted_iota(jnp.int32, (DIL_SUBQ, nkeys), 0)
    c = lax.broadcasted_iota(jnp.int32, (DIL_SUBQ, nkeys), 1)
    band = (c >= a) & (c <= a + 2 * HALF_WIN)
    lane = lax.broadcasted_iota(jnp.int32, (DIL_SUBQ, LANES), 1)

    def window(p_ref, c_ref, n_ref, j, hs):
        lo = j * DIL_SUBQ - HALF_WIN
        hi = lo + nkeys
        parts = []
        if lo < 0:
            parts.append(p_ref[0, :, hs])
        parts.append(c_ref[0, max(lo, 0):min(hi, bq), hs])
        if hi > bq:
            parts.append(n_ref[0, :, hs])
        return jnp.concatenate(parts, axis=0) if len(parts) > 1 else parts[0]

    for j in range(n_sub):
        mask = band
        if j == 0:
            mask = mask & ((c >= HALF_WIN) | (i > 0))
        if j == n_sub - 1:
            mask = mask & ((c < DIL_SUBQ + HALF_WIN) | (i < nblk - 1))
        rows = slice(j * DIL_SUBQ, (j + 1) * DIL_SUBQ)
        lse_all = jnp.zeros((DIL_SUBQ, LANES), F32)
        for h in range(B_HEADS):
            hs = slice(h * B_HDIM, (h + 1) * B_HDIM)
            k = window(kp_ref, kc_ref, kn_ref, j, hs)
            v = window(vp_ref, vc_ref, vn_ref, j, hs)
            sc = jnp.where(mask, _nt_dot(q_ref[0, rows, hs], k), NEG_BIG)
            m = jnp.max(sc, axis=1, keepdims=True)
            p = jnp.exp2(sc - m)
            den = jnp.sum(p, axis=1, keepdims=True)
            num = jnp.dot(p.astype(BF16), v, preferred_element_type=F32)
            o_ref[0, rows, hs] = num * (1.0 / den)
            lse_all = jnp.where(lane == h, m + jnp.log2(den), lse_all)
        lse_ref[0, rows, :] = lse_all


def _dilated(qv, kv, vv, b, dil, bq):
    m_len = qv.shape[0] // b
    nblk = m_len // bq
    sub = bq // HALF_WIN
    n_half = m_len // HALF_WIN
    view = lambda t: t.reshape(b, m_len, dil * B_WIDTH)
    cur = pl.BlockSpec((1, bq, B_WIDTH), lambda bi, r, i: (bi, i, r))
    prev = pl.BlockSpec((1, HALF_WIN, B_WIDTH), lambda bi, r, i: (bi, jnp.maximum(i * sub - 1, 0), r))
    nxt = pl.BlockSpec((1, HALF_WIN, B_WIDTH), lambda bi, r, i: (bi, jnp.minimum((i + 1) * sub, n_half - 1), r))
    kern = functools.partial(_dilated_kernel, bq=bq, nblk=nblk)
    o, lse = pl.pallas_call(
        kern,
        grid=(b, dil, nblk),
        in_specs=[cur, prev, cur, nxt, prev, cur, nxt],
        out_specs=[cur, pl.BlockSpec((1, bq, LANES), lambda bi, r, i: (bi, i, r))],
        out_shape=[jax.ShapeDtypeStruct((b, m_len, dil * B_WIDTH), F32),
                   jax.ShapeDtypeStruct((b, m_len, dil * LANES), F32)],
        compiler_params=_cparams(("parallel", "parallel", "parallel")),
        name=f"dilated_d{dil}",
    )(view(qv), view(kv), view(kv), view(kv), view(vv), view(vv), view(vv))
    return o.reshape(b * m_len, dil * B_WIDTH), lse.reshape(b * m_len, dil * LANES)


def _out_proj_kernel(oa_ref, o1_ref, o2_ref, o3_ref, l1_ref, l2_ref, l3_ref, ga_ref, gb_ref, x_ref,
                     w_ref, goa_ref, gob_ref, lng_ref, lnb_ref, y_ref, on_ref, ln_ref):
    za = (_rms(oa_ref[...], goa_ref[...]) * ga_ref[...].astype(F32)).astype(BF16)
    tm = oa_ref.shape[0]
    for i, (o_ref, l_ref, (_, dil)) in enumerate(zip((o1_ref, o2_ref, o3_ref), (l1_ref, l2_ref, l3_ref),
                                                     DILATED_PATTERNS)):
        for r in range(dil):
            rows = pl.ds(r, tm // dil, stride=dil) if dil > 1 else slice(None)
            ln_ref[i, rows, :] = l_ref[:, r * LANES:(r + 1) * LANES]
            for h in range(B_HEADS):
                col = r * B_WIDTH + h * B_HDIM
                on_ref[i, h, rows, :] = o_ref[:, col:col + B_HDIM]
    l1, l2, l3 = ln_ref[0], ln_ref[1], ln_ref[2]
    mx = jnp.maximum(jnp.maximum(l1, l2), l3)
    e1, e2, e3 = jnp.exp2(l1 - mx), jnp.exp2(l2 - mx), jnp.exp2(l3 - mx)
    inv = 1.0 / (e1 + e2 + e3)
    w1, w2, w3 = e1 * inv, e2 * inv, e3 * inv
    parts = []
    for h in range(B_HEADS):
        parts.append(w1[:, h:h + 1] * on_ref[0, h] + w2[:, h:h + 1] * on_ref[1, h] + w3[:, h:h + 1] * on_ref[2, h])
    ob = jnp.concatenate(parts, axis=1)
    zb = (_rms(ob, gob_ref[...]) * gb_ref[...].astype(F32)).astype(BF16)
    out = (jnp.dot(za, w_ref[:A_WIDTH, :], preferred_element_type=F32)
           + jnp.dot(zb, w_ref[A_WIDTH:, :], preferred_element_type=F32))
    r = DEEPNORM_ALPHA * x_ref[...] + out
    mu = jnp.mean(r, axis=-1, keepdims=True)
    d = r - mu
    var = jnp.mean(d * d, axis=-1, keepdims=True)
    y_ref[...] = d * lax.rsqrt(var + LN_EPS) * lng_ref[...] + lnb_ref[...]


def _out_proj(oa, obs, lses, ga, gb, x2, w_out, goa, gob, lng, lnb, tm):
    n = x2.shape[0]
    row = lambda w: pl.BlockSpec((tm, w), lambda i: (i, 0))
    view = lambda w: [pl.BlockSpec((tm // d, d * w), lambda i: (i, 0)) for _, d in DILATED_PATTERNS]
    full = lambda *shape: pl.BlockSpec(shape, lambda i: (0,) * len(shape))
    n_pat = len(DILATED_PATTERNS)
    return pl.pallas_call(
        _out_proj_kernel,
        grid=(n // tm,),
        in_specs=[row(A_WIDTH)] + view(B_WIDTH) + view(LANES) + [row(A_WIDTH), row(B_WIDTH), row(D_MODEL),
                  full(A_WIDTH + B_WIDTH, D_MODEL), full(1, A_WIDTH), full(1, B_WIDTH),
                  full(1, D_MODEL), full(1, D_MODEL)],
        out_specs=row(D_MODEL),
        out_shape=jax.ShapeDtypeStruct((n, D_MODEL), F32),
        scratch_shapes=[pltpu.VMEM((n_pat, B_HEADS, tm, B_HDIM), F32), pltpu.VMEM((n_pat, tm, LANES), F32)],
        compiler_params=_cparams(("parallel",)),
        name="out_proj",
    )(oa, *obs, *lses, ga, gb, x2, w_out, goa, gob, lng, lnb)


def _rope_tables(s):
    pos = jnp.arange(s, dtype=F32)

    def cos_sin(d):
        half = d // 2
        inv = ROPE_THETA ** (-jnp.arange(half, dtype=F32) * 2.0 / d)
        ang = pos[:, None] * inv[None, :]
        return jnp.cos(ang), jnp.sin(ang)

    cb, sb = cos_sin(B_HDIM)
    cos_b = jnp.concatenate([cb, cb], axis=1)
    sin_b = jnp.concatenate([-sb, sb], axis=1)
    ca, sa = cos_sin(A_ROPE)
    zeros = jnp.zeros((s, LANES - A_ROPE), F32)
    cos_k = jnp.concatenate([ca, ca, zeros], axis=1)
    sin_k = jnp.concatenate([-sa, sa, zeros], axis=1)
    return cos_b, sin_b, cos_k, sin_k, ca.T, sa.T


def _prep_weights(w_in, g_qn, w_uq, g_kvn, w_ukv, g_oa, g_ob, w_out, ln_g, ln_b):
    n_lat = Q_LORA + KV_LORA + A_ROPE
    w_seg = jnp.concatenate([w_in[:, :n_lat], jnp.zeros((D_MODEL, LAT_PAD), w_in.dtype), w_in[:, n_lat:]],
                            axis=1).astype(BF16)
    wuqt = w_uq.reshape(Q_LORA, A_HEADS, A_QK).transpose(1, 2, 0).astype(BF16)
    wukv = w_ukv.reshape(KV_LORA, A_HEADS, A_NOPE + A_VDIM)
    wuk = wukv[:, :, :A_NOPE].transpose(1, 0, 2).astype(BF16)
    wuvt = wukv[:, :, A_NOPE:].transpose(1, 2, 0).astype(BF16)
    return dict(w_seg=w_seg, gq=g_qn[None, :], gkv=g_kvn[None, :], wuqt=wuqt, wuk=wuk, wuvt=wuvt,
                goa=g_oa[None, :], gob=g_ob[None, :], w_out=w_out.astype(BF16),
                lng=ln_g[None, :], lnb=ln_b[None, :])


def _tiles(s):
    pick = lambda pref, n: pref if n % pref == 0 else n
    return dict(tm_in=pick(512, s), tm_up=pick(512, s), bq=pick(1024, s), bk=pick(256, s), tm_out=pick(256, s))


def _encoder_layer(x, p):
    b, s, _ = x.shape
    longest = max(w // 2 for w, _ in DILATED_PATTERNS)
    assert s % longest == 0, "sequence length must be a multiple of the largest dilated chunk"
    t = _tiles(s)
    cos_b, sin_b, cos_k, sin_k, cos_t, sin_t = _rope_tables(s)
    x2 = x.reshape(b * s, D_MODEL)
    tm = t["tm_in"]
    w_seg = p["w_seg"]
    lat = _in_proj(x2, w_seg, 0, "lat", tm)
    qvs = _in_proj(x2, w_seg, 1, "rope", tm, s, cos_b, sin_b, B_SCALE)
    kvs = _in_proj(x2, w_seg, 2, "rope", tm, s, cos_b, sin_b)
    vvs = _in_proj(x2, w_seg, 3, "plain", tm)
    ga = _in_proj(x2, w_seg, 4, "silu", tm)
    gb = _in_proj(x2, w_seg, 5, "silu", tm)
    qt, k, vt = _mla_up(lat.reshape(b, s, SEG), p["gq"], p["gkv"], p["wuqt"], p["wuk"], p["wuvt"],
                        cos_k, sin_k, cos_t, sin_t, t["tm_up"])
    oa = _mla_flash(qt, k, vt, t["bq"], t["bk"]).reshape(b * s, A_WIDTH)
    obs, lses = [], []
    for i, (window, dil) in enumerate(DILATED_PATTERNS):
        assert window // (2 * dil) == HALF_WIN
        o, lse = _dilated(qvs[i], kvs[i], vvs[i], b, dil, min(DIL_BQ, s // dil))
        obs.append(o)
        lses.append(lse)
    y = _out_proj(oa, obs, lses, ga, gb, x2, p["w_out"], p["goa"], p["gob"], p["lng"], p["lnb"], t["tm_out"])
    return y.reshape(b, s, D_MODEL)


def kernel(x_prompt, x_sample, w_in, g_qn, w_uq, g_kvn, w_ukv, g_oa, g_ob, w_out, ln_g, ln_b):
    def trunk(x):
        for l in range(DEPTH):
            p = _prep_weights(w_in[l], g_qn[l], w_uq[l], g_kvn[l], w_ukv[l], g_oa[l], g_ob[l],
                              w_out[l], ln_g[l], ln_b[l])
            x = _encoder_layer(x, p)
        return x

    return (trunk(x_prompt), trunk(x_sample))
```

```python
import functools

import jax
import jax.numpy as jnp
from jax import lax
from jax.experimental import pallas as pl
from jax.experimental.pallas import tpu as pltpu

F32 = jnp.float32
BF16 = jnp.bfloat16

D_MODEL = 2048
DEPTH = 1
A_HEADS = 8
A_NOPE = 128
A_ROPE = 64
A_VDIM = 128
A_QK = A_NOPE + A_ROPE
A_WIDTH = A_HEADS * A_VDIM
A_VROWS = A_VDIM + 16
Q_LORA = 512
KV_LORA = 256
B_HEADS = 8
B_HDIM = 128
B_WIDTH = B_HEADS * B_HDIM
DILATED_PATTERNS = ((128, 1), (512, 4), (2048, 16))
ROPE_THETA = 10000.0
RMS_EPS = 1e-6
LN_EPS = 1e-5
NEG_BIG = -1e30
DEEPNORM_ALPHA = (2 * DEPTH) ** 0.25
LOG2E = 1.4426950408889634
A_SCALE = A_QK ** -0.5 * LOG2E
B_SCALE = B_HDIM ** -0.5 * LOG2E

LANES = 128
SEG = 1024
N_SEG = 6
LAT_PAD = SEG - (Q_LORA + KV_LORA + A_ROPE)
HALF_WIN = 64
DIL_SUBQ = 128
DIL_BQ = 512
VMEM_LIMIT = 56 * 1024 * 1024


def _cparams(semantics, flags=None):
    return pltpu.CompilerParams(dimension_semantics=semantics, vmem_limit_bytes=VMEM_LIMIT, flags=flags)


def _nt_dot(a, b):
    return lax.dot_general(a, b, (((1,), (1,)), ((), ())), preferred_element_type=F32)


SEG_MODES = ("lat", "rope", "rope", "plain", "silu", "silu")
SEG_SCALES = (1.0, B_SCALE, 1.0, 1.0, 1.0, 1.0)
N_DIL_OPERANDS = sum(m in ("rope", "plain") for m in SEG_MODES)


def _in_proj_kernel(x_ref, w_ref, cos_ref, sin_ref, *refs):
    outs, seg_ref = list(refs[:-1]), refs[-1]
    tm = x_ref.shape[0]
    xb = x_ref[...].astype(BF16)
    cos = cos_ref[...]
    sin = sin_ref[...]
    n_dil = 0
    for j, (mode, scale) in enumerate(zip(SEG_MODES, SEG_SCALES)):
        acc = jnp.dot(xb, w_ref[:, j * SEG:(j + 1) * SEG], preferred_element_type=F32)
        if mode == "lat":
            outs.pop(0)[...] = acc
            continue
        if mode == "silu":
            outs.pop(0)[...] = (acc * jax.nn.sigmoid(acc)).astype(BF16)
            continue
        pattern_outs = [outs.pop(0) for _ in DILATED_PATTERNS]
        for h in range(B_HEADS):
            xh = acc[:, h * B_HDIM:(h + 1) * B_HDIM]
            if mode == "rope":
                xh = (xh * cos + pltpu.roll(xh, B_HDIM // 2, 1) * sin) * scale
            seg_ref[n_dil, h] = xh
            for o_ref, (_, dil) in zip(pattern_outs, DILATED_PATTERNS):
                if dil == 1:
                    o_ref[:, h * B_HDIM:(h + 1) * B_HDIM] = xh.astype(BF16)
                    continue
                for r in range(dil):
                    col = r * SEG + h * B_HDIM
                    o_ref[:, col:col + B_HDIM] = seg_ref[n_dil, h, pl.ds(r, tm // dil, stride=dil), :].astype(BF16)
        n_dil += 1


def _in_proj(x2, w_seg, cos_b, sin_b, s, tm):
    n = x2.shape[0]
    n_pos = s // tm
    row = lambda i: (i, 0)
    out_specs, out_shape = [], []
    for mode in SEG_MODES:
        if mode in ("lat", "silu"):
            out_specs.append(pl.BlockSpec((tm, SEG), row))
            out_shape.append(jax.ShapeDtypeStruct((n, SEG), F32 if mode == "lat" else BF16))
        else:
            out_specs += [pl.BlockSpec((tm // d, d * SEG), row) for _, d in DILATED_PATTERNS]
            out_shape += [jax.ShapeDtypeStruct((n // d, d * SEG), BF16) for _, d in DILATED_PATTERNS]
    flat = pl.pallas_call(
        _in_proj_kernel,
        grid=(n // tm,),
        in_specs=[
            pl.BlockSpec((tm, D_MODEL), row),
            pl.BlockSpec((D_MODEL, N_SEG * SEG), lambda i: (0, 0), pipeline_mode=pl.Buffered(1)),
            pl.BlockSpec((tm, B_HDIM), lambda i: (i % n_pos, 0)),
            pl.BlockSpec((tm, B_HDIM), lambda i: (i % n_pos, 0)),
        ],
        out_specs=out_specs,
        out_shape=out_shape,
        scratch_shapes=[pltpu.VMEM((N_DIL_OPERANDS, B_HEADS, tm, B_HDIM), F32)],
        compiler_params=_cparams(("parallel",)),
        name="in_proj",
    )(x2, w_seg, cos_b, sin_b)
    flat = list(flat)
    result = []
    for mode in SEG_MODES:
        if mode in ("lat", "silu"):
            result.append(flat.pop(0))
        else:
            result.append(tuple(flat.pop(0) for _ in DILATED_PATTERNS))
    return result


def _rms(x, g):
    return x * lax.rsqrt(jnp.mean(x * x, axis=-1, keepdims=True) + RMS_EPS) * g


def _mla_up_kernel(lat_ref, gq_ref, gkv_ref, wuqt_ref, wuk_ref, wuvt_ref,
                   cosk_ref, sink_ref, cost_ref, sint_ref, qt_ref, k_ref, vt_ref):
    lat = lat_ref[0]
    cqn = _rms(lat[:, :Q_LORA], gq_ref[...]).astype(BF16)
    ckvn = _rms(lat[:, Q_LORA:Q_LORA + KV_LORA], gkv_ref[...]).astype(BF16)
    kr = lat[:, Q_LORA + KV_LORA:Q_LORA + KV_LORA + LANES]
    lane = lax.broadcasted_iota(jnp.int32, kr.shape, 1)
    half = A_ROPE // 2
    rot = jnp.where(lane < half, pltpu.roll(kr, LANES - half, 1), pltpu.roll(kr, half, 1))
    kpe = (kr * cosk_ref[...] + rot * sink_ref[...]).astype(BF16)
    ct = cost_ref[...]
    st = sint_ref[...]
    for h in range(A_HEADS):
        qt = _nt_dot(wuqt_ref[h], cqn)
        x1 = qt[A_NOPE:A_NOPE + half]
        x2 = qt[A_NOPE + half:]
        qt_ref[0, h, 0:A_NOPE, :] = (qt[:A_NOPE] * A_SCALE).astype(BF16)
        qt_ref[0, h, A_NOPE:A_NOPE + half, :] = ((x1 * ct - x2 * st) * A_SCALE).astype(BF16)
        qt_ref[0, h, A_NOPE + half:, :] = ((x2 * ct + x1 * st) * A_SCALE).astype(BF16)
        k_ref[0, h, :, 0:A_NOPE] = jnp.dot(ckvn, wuk_ref[h], preferred_element_type=F32).astype(BF16)
        k_ref[0, h, :, A_NOPE:] = kpe[:, :A_ROPE]
        vt_ref[0, h, 0:A_VDIM, :] = _nt_dot(wuvt_ref[h], ckvn).astype(BF16)
        vt_ref[0, h, A_VDIM:, :] = jnp.ones((A_VROWS - A_VDIM, vt_ref.shape[3]), BF16)


def _mla_up(lat3, gq, gkv, wuqt, wuk, wuvt, cosk, sink, cost, sint, tm):
    b, s, _ = lat3.shape
    full = lambda *shape: pl.BlockSpec(shape, lambda bi, si: (0,) * len(shape))
    return pl.pallas_call(
        _mla_up_kernel,
        grid=(b, s // tm),
        in_specs=[
            pl.BlockSpec((1, tm, SEG), lambda bi, si: (bi, si, 0)),
            full(1, Q_LORA), full(1, KV_LORA),
            full(A_HEADS, A_QK, Q_LORA), full(A_HEADS, KV_LORA, A_NOPE), full(A_HEADS, A_VDIM, KV_LORA),
            pl.BlockSpec((tm, LANES), lambda bi, si: (si, 0)),
            pl.BlockSpec((tm, LANES), lambda bi, si: (si, 0)),
            pl.BlockSpec((A_ROPE // 2, tm), lambda bi, si: (0, si)),
            pl.BlockSpec((A_ROPE // 2, tm), lambda bi, si: (0, si)),
        ],
        out_specs=[
            pl.BlockSpec((1, A_HEADS, A_QK, tm), lambda bi, si: (bi, 0, 0, si)),
            pl.BlockSpec((1, A_HEADS, tm, A_QK), lambda bi, si: (bi, 0, si, 0)),
            pl.BlockSpec((1, A_HEADS, A_VROWS, tm), lambda bi, si: (bi, 0, 0, si)),
        ],
        out_shape=[
            jax.ShapeDtypeStruct((b, A_HEADS, A_QK, s), BF16),
            jax.ShapeDtypeStruct((b, A_HEADS, s, A_QK), BF16),
            jax.ShapeDtypeStruct((b, A_HEADS, A_VROWS, s), BF16),
        ],
        compiler_params=_cparams(("parallel", "parallel")),
        name="mla_up",
    )(lat3, gq, gkv, wuqt, wuk, wuvt, cosk, sink, cost, sint)


FLASH_BUFS = 3
FLASH_UNROLL = 6
FLASH_FLAGS = None


def _mla_flash_kernel(qt_ref, k_ref, vt_ref, o_ref, s_ref, p_ref, acc_ref, *, bk, nk):
    bq = qt_ref.shape[3]

    def scores(t, slot):
        start = pl.multiple_of(t * bk, bk)
        st = jnp.dot(k_ref[0, 0, pl.ds(start, bk), :], qt_ref[0, 0], preferred_element_type=F32)
        s_ref[slot] = st
        return jnp.max(st, axis=0, keepdims=True)

    def softmax(slot, m, cmax):
        m_new = jnp.maximum(m, cmax)
        p_ref[slot] = jnp.exp2((s_ref[slot] - m_new).astype(BF16))
        return m_new, jnp.exp2(m - m_new)

    def values(t, slot, alpha):
        start = pl.multiple_of(t * bk, bk)
        vt = vt_ref[0, 0, :, pl.ds(start, bk)]
        acc_ref[...] = alpha * acc_ref[...] + jnp.dot(vt, p_ref[slot], preferred_element_type=F32)

    def step(t, t_slot, carry, last=False):
        m, alpha, cmax = carry
        cmax_next = cmax if last else scores(t + 1, (t_slot + 1) % FLASH_BUFS)
        m, alpha_t = softmax(t_slot, m, cmax)
        values(t - 1, (t_slot - 1) % FLASH_BUFS, alpha)
        return m, alpha_t, cmax_next

    acc_ref[...] = jnp.zeros(acc_ref.shape, F32)
    cmax0 = scores(0, 0)
    cmax1 = scores(1, 1)
    carry = softmax(0, jnp.full((1, bq), NEG_BIG, F32), cmax0) + (cmax1,)

    def group(j, carry):
        for c in range(FLASH_UNROLL):
            carry = step(FLASH_UNROLL * j + 1 + c, (1 + c) % FLASH_BUFS, carry)
        return carry

    n_groups = (nk - 2) // FLASH_UNROLL
    carry = lax.fori_loop(0, n_groups, group, carry)
    for t in range(FLASH_UNROLL * n_groups + 1, nk):
        carry = step(t, t % FLASH_BUFS, carry, last=(t == nk - 1))
    values(nk - 1, (nk - 1) % FLASH_BUFS, carry[1])
    acc = acc_ref[...]
    o = acc[:A_VDIM] * (1.0 / acc[A_VDIM:A_VDIM + 1])
    o_ref[0] = o.T


def _mla_flash(qt, k, vt, bq, bk):
    b, h, _, s = qt.shape
    nk = s // bk
    assert nk >= 2
    kern = functools.partial(_mla_flash_kernel, bk=bk, nk=nk)
    return pl.pallas_call(
        kern,
        grid=(b, h, s // bq),
        in_specs=[
            pl.BlockSpec((1, 1, A_QK, bq), lambda bi, hi, qi: (bi, hi, 0, qi)),
            pl.BlockSpec((1, 1, s, A_QK), lambda bi, hi, qi: (bi, hi, 0, 0)),
            pl.BlockSpec((1, 1, A_VROWS, s), lambda bi, hi, qi: (bi, hi, 0, 0)),
        ],
        out_specs=pl.BlockSpec((1, bq, A_VDIM), lambda bi, hi, qi: (bi, qi, hi)),
        out_shape=jax.ShapeDtypeStruct((b, s, A_WIDTH), F32),
        scratch_shapes=[pltpu.VMEM((FLASH_BUFS, bk, bq), F32), pltpu.VMEM((FLASH_BUFS, bk, bq), BF16),
                        pltpu.VMEM((A_VROWS, bq), F32)],
        compiler_params=_cparams(("parallel", "parallel", "arbitrary"), FLASH_FLAGS),
        name="mla_flash",
    )(qt, k, vt)


def _dilated_kernel(q_ref, kp_ref, kc_ref, kn_ref, vp_ref, vc_ref, vn_ref, o_ref, lse_ref, *, bq, nblk):
    i = pl.program_id(2)
    nkeys = DIL_SUBQ + 2 * HALF_WIN
    n_sub = bq // DIL_SUBQ
    a = lax.broadcasted_iota(jnp.int32, (DIL_SUBQ, nkeys), 0)
    c = lax.broadcasted_iota(jnp.int32, (DIL_SUBQ, nkeys), 1)
    band = (c >= a) & (c <= a + 2 * HALF_WIN)
    lane = lax.broadcasted_iota(jnp.int32, (DIL_SUBQ, LANES), 1)

    def window(p_ref, c_ref, n_ref, j, hs):
        lo = j * DIL_SUBQ - HALF_WIN
        hi = lo + nkeys
        parts = []
        if lo < 0:
            parts.append(p_ref[0, :, hs])
        parts.append(c_ref[0, max(lo, 0):min(hi, bq), hs])
        if hi > bq:
            parts.append(n_ref[0, :, hs])
        return jnp.concatenate(parts, axis=0) if len(parts) > 1 else parts[0]

    for j in range(n_sub):
        mask = band
        if j == 0:
            mask = mask & ((c >= HALF_WIN) | (i > 0))
        if j == n_sub - 1:
            mask = mask & ((c < DIL_SUBQ + HALF_WIN) | (i < nblk - 1))
        rows = slice(j * DIL_SUBQ, (j + 1) * DIL_SUBQ)
        lse_all = jnp.zeros((DIL_SUBQ, LANES), F32)
        for h in range(B_HEADS):
            hs = slice(h * B_HDIM, (h + 1) * B_HDIM)
            k = window(kp_ref, kc_ref, kn_ref, j, hs)
            v = window(vp_ref, vc_ref, vn_ref, j, hs)
            sc = jnp.where(mask, _nt_dot(q_ref[0, rows, hs], k), NEG_BIG)
            m = jnp.max(sc, axis=1, keepdims=True)
            p = jnp.exp2(sc - m)
            den = jnp.sum(p, axis=1, keepdims=True)
            num = jnp.dot(p.astype(BF16), v, preferred_element_type=F32)
            o_ref[0, rows, hs] = num * (1.0 / den)
            lse_all = jnp.where(lane == h, m + jnp.log2(den), lse_all)
        lse_ref[0, rows, :] = lse_all


def _dilated(qv, kv, vv, b, dil, bq):
    m_len = qv.shape[0] // b
    nblk = m_len // bq
    sub = bq // HALF_WIN
    n_half = m_len // HALF_WIN
    view = lambda t: t.reshape(b, m_len, dil * B_WIDTH)
    cur = pl.BlockSpec((1, bq, B_WIDTH), lambda bi, r, i: (bi, i, r))
    prev = pl.BlockSpec((1, HALF_WIN, B_WIDTH), lambda bi, r, i: (bi, jnp.maximum(i * sub - 1, 0), r))
    nxt = pl.BlockSpec((1, HALF_WIN, B_WIDTH), lambda bi, r, i: (bi, jnp.minimum((i + 1) * sub, n_half - 1), r))
    kern = functools.partial(_dilated_kernel, bq=bq, nblk=nblk)
    o, lse = pl.pallas_call(
        kern,
        grid=(b, dil, nblk),
        in_specs=[cur, prev, cur, nxt, prev, cur, nxt],
        out_specs=[cur, pl.BlockSpec((1, bq, LANES), lambda bi, r, i: (bi, i, r))],
        out_shape=[jax.ShapeDtypeStruct((b, m_len, dil * B_WIDTH), F32),
                   jax.ShapeDtypeStruct((b, m_len, dil * LANES), F32)],
        compiler_params=_cparams(("parallel", "parallel", "parallel")),
        name=f"dilated_d{dil}",
    )(view(qv), view(kv), view(kv), view(kv), view(vv), view(vv), view(vv))
    return o.reshape(b * m_len, dil * B_WIDTH), lse.reshape(b * m_len, dil * LANES)


def _out_proj_kernel(oa_ref, o1_ref, o2_ref, o3_ref, l1_ref, l2_ref, l3_ref, ga_ref, gb_ref, x_ref,
                     w_ref, goa_ref, gob_ref, lng_ref, lnb_ref, y_ref, on_ref, ln_ref):
    za = (_rms(oa_ref[...], goa_ref[...]) * ga_ref[...].astype(F32)).astype(BF16)
    tm = oa_ref.shape[0]
    for i, (o_ref, l_ref, (_, dil)) in enumerate(zip((o1_ref, o2_ref, o3_ref), (l1_ref, l2_ref, l3_ref),
                                                     DILATED_PATTERNS)):
        for r in range(dil):
            rows = pl.ds(r, tm // dil, stride=dil) if dil > 1 else slice(None)
            ln_ref[i, rows, :] = l_ref[:, r * LANES:(r + 1) * LANES]
            for h in range(B_HEADS):
                col = r * B_WIDTH + h * B_HDIM
                on_ref[i, h, rows, :] = o_ref[:, col:col + B_HDIM]
    l1, l2, l3 = ln_ref[0], ln_ref[1], ln_ref[2]
    mx = jnp.maximum(jnp.maximum(l1, l2), l3)
    e1, e2, e3 = jnp.exp2(l1 - mx), jnp.exp2(l2 - mx), jnp.exp2(l3 - mx)
    inv = 1.0 / (e1 + e2 + e3)
    w1, w2, w3 = e1 * inv, e2 * inv, e3 * inv
    parts = []
    for h in range(B_HEADS):
        parts.append(w1[:, h:h + 1] * on_ref[0, h] + w2[:, h:h + 1] * on_ref[1, h] + w3[:, h:h + 1] * on_ref[2, h])
    ob = jnp.concatenate(parts, axis=1)
    zb = (_rms(ob, gob_ref[...]) * gb_ref[...].astype(F32)).astype(BF16)
    out = (jnp.dot(za, w_ref[:A_WIDTH, :], preferred_element_type=F32)
           + jnp.dot(zb, w_ref[A_WIDTH:, :], preferred_element_type=F32))
    r = DEEPNORM_ALPHA * x_ref[...] + out
    mu = jnp.mean(r, axis=-1, keepdims=True)
    d = r - mu
    var = jnp.mean(d * d, axis=-1, keepdims=True)
    y_ref[...] = d * lax.rsqrt(var + LN_EPS) * lng_ref[...] + lnb_ref[...]


def _out_proj(oa, obs, lses, ga, gb, x2, w_out, goa, gob, lng, lnb, tm):
    n = x2.shape[0]
    row = lambda w: pl.BlockSpec((tm, w), lambda i: (i, 0))
    view = lambda w: [pl.BlockSpec((tm // d, d * w), lambda i: (i, 0)) for _, d in DILATED_PATTERNS]
    full = lambda *shape: pl.BlockSpec(shape, lambda i: (0,) * len(shape))
    n_pat = len(DILATED_PATTERNS)
    return pl.pallas_call(
        _out_proj_kernel,
        grid=(n // tm,),
        in_specs=[row(A_WIDTH)] + view(B_WIDTH) + view(LANES) + [row(A_WIDTH), row(B_WIDTH), row(D_MODEL),
                  full(A_WIDTH + B_WIDTH, D_MODEL), full(1, A_WIDTH), full(1, B_WIDTH),
                  full(1, D_MODEL), full(1, D_MODEL)],
        out_specs=row(D_MODEL),
        out_shape=jax.ShapeDtypeStruct((n, D_MODEL), F32),
        scratch_shapes=[pltpu.VMEM((n_pat, B_HEADS, tm, B_HDIM), F32), pltpu.VMEM((n_pat, tm, LANES), F32)],
        compiler_params=_cparams(("parallel",)),
        name="out_proj",
    )(oa, *obs, *lses, ga, gb, x2, w_out, goa, gob, lng, lnb)


def _rope_tables(s):
    pos = jnp.arange(s, dtype=F32)

    def cos_sin(d):
        half = d // 2
        inv = ROPE_THETA ** (-jnp.arange(half, dtype=F32) * 2.0 / d)
        ang = pos[:, None] * inv[None, :]
        return jnp.cos(ang), jnp.sin(ang)

    cb, sb = cos_sin(B_HDIM)
    cos_b = jnp.concatenate([cb, cb], axis=1)
    sin_b = jnp.concatenate([-sb, sb], axis=1)
    ca, sa = cos_sin(A_ROPE)
    zeros = jnp.zeros((s, LANES - A_ROPE), F32)
    cos_k = jnp.concatenate([ca, ca, zeros], axis=1)
    sin_k = jnp.concatenate([-sa, sa, zeros], axis=1)
    return cos_b, sin_b, cos_k, sin_k, ca.T, sa.T


def _prep_weights(w_in, g_qn, w_uq, g_kvn, w_ukv, g_oa, g_ob, w_out, ln_g, ln_b):
    n_lat = Q_LORA + KV_LORA + A_ROPE
    w_seg = jnp.concatenate([w_in[:, :n_lat], jnp.zeros((D_MODEL, LAT_PAD), w_in.dtype), w_in[:, n_lat:]],
                            axis=1).astype(BF16)
    wuqt = w_uq.reshape(Q_LORA, A_HEADS, A_QK).transpose(1, 2, 0).astype(BF16)
    wukv = w_ukv.reshape(KV_LORA, A_HEADS, A_NOPE + A_VDIM)
    wuk = wukv[:, :, :A_NOPE].transpose(1, 0, 2).astype(BF16)
    wuvt = wukv[:, :, A_NOPE:].transpose(1, 2, 0).astype(BF16)
    return dict(w_seg=w_seg, gq=g_qn[None, :], gkv=g_kvn[None, :], wuqt=wuqt, wuk=wuk, wuvt=wuvt,
                goa=g_oa[None, :], gob=g_ob[None, :], w_out=w_out.astype(BF16),
                lng=ln_g[None, :], lnb=ln_b[None, :])


def _tiles(s):
    pick = lambda pref, n: pref if n % pref == 0 else n
    return dict(tm_in=pick(256, s), tm_up=pick(512, s), bq=pick(1024, s), bk=pick(256, s), tm_out=pick(256, s))


def _encoder_layer(x, p):
    b, s, _ = x.shape
    longest = max(w // 2 for w, _ in DILATED_PATTERNS)
    assert s % longest == 0, "sequence length must be a multiple of the largest dilated chunk"
    t = _tiles(s)
    cos_b, sin_b, cos_k, sin_k, cos_t, sin_t = _rope_tables(s)
    x2 = x.reshape(b * s, D_MODEL)
    lat, qvs, kvs, vvs, ga, gb = _in_proj(x2, p["w_seg"], cos_b, sin_b, s, t["tm_in"])
    qt, k, vt = _mla_up(lat.reshape(b, s, SEG), p["gq"], p["gkv"], p["wuqt"], p["wuk"], p["wuvt"],
                        cos_k, sin_k, cos_t, sin_t, t["tm_up"])
    oa = _mla_flash(qt, k, vt, t["bq"], t["bk"]).reshape(b * s, A_WIDTH)
    obs, lses = [], []
    for i, (window, dil) in enumerate(DILATED_PATTERNS):
        assert window // (2 * dil) == HALF_WIN
        o, lse = _dilated(qvs[i], kvs[i], vvs[i], b, dil, min(DIL_BQ, s // dil))
        obs.append(o)
        lses.append(lse)
    y = _out_proj(oa, obs, lses, ga, gb, x2, p["w_out"], p["goa"], p["gob"], p["lng"], p["lnb"], t["tm_out"])
    return y.reshape(b, s, D_MODEL)


def kernel(x_prompt, x_sample, w_in, g_qn, w_uq, g_kvn, w_ukv, g_oa, g_ob, w_out, ln_g, ln_b):
    def trunk(x):
        for l in range(DEPTH):
            p = _prep_weights(w_in[l], g_qn[l], w_uq[l], g_kvn[l], w_ukv[l], g_oa[l], g_ob[l],
                              w_out[l], ln_g[l], ln_b[l])
            x = _encoder_layer(x, p)
        return x

    return (trunk(x_prompt), trunk(x_sample))
```

```python
import functools

import jax
import jax.numpy as jnp
from jax import lax
from jax.experimental import pallas as pl
from jax.experimental.pallas import tpu as pltpu

F32 = jnp.float32
BF16 = jnp.bfloat16

D_MODEL = 2048
DEPTH = 1
A_HEADS = 8
A_NOPE = 128
A_ROPE = 64
A_VDIM = 128
A_QK = A_NOPE + A_ROPE
A_WIDTH = A_HEADS * A_VDIM
A_VROWS = A_VDIM + 16
Q_LORA = 512
KV_LORA = 256
B_HEADS = 8
B_HDIM = 128
B_WIDTH = B_HEADS * B_HDIM
DILATED_PATTERNS = ((128, 1), (512, 4), (2048, 16))
ROPE_THETA = 10000.0
RMS_EPS = 1e-6
LN_EPS = 1e-5
NEG_BIG = -1e30
DEEPNORM_ALPHA = (2 * DEPTH) ** 0.25
LOG2E = 1.4426950408889634
A_SCALE = A_QK ** -0.5 * LOG2E
B_SCALE = B_HDIM ** -0.5 * LOG2E

LANES = 128
SEG = 1024
N_SEG = 6
LAT_PAD = SEG - (Q_LORA + KV_LORA + A_ROPE)
HALF_WIN = 64
DIL_SUBQ = 128
DIL_BQ = 512
VMEM_LIMIT = 56 * 1024 * 1024


def _cparams(semantics, flags=None):
    return pltpu.CompilerParams(dimension_semantics=semantics, vmem_limit_bytes=VMEM_LIMIT, flags=flags)


def _nt_dot(a, b):
    return lax.dot_general(a, b, (((1,), (1,)), ((), ())), preferred_element_type=F32)


SEG_MODES = ("lat", "rope", "rope", "plain", "silu", "silu")
SEG_SCALES = (1.0, B_SCALE, 1.0, 1.0, 1.0, 1.0)
N_DIL_OPERANDS = sum(m in ("rope", "plain") for m in SEG_MODES)


def _in_proj_kernel(x_ref, w_ref, cos_ref, sin_ref, *refs):
    outs, seg_ref = list(refs[:-1]), refs[-1]
    tm = x_ref.shape[0]
    xb = x_ref[...].astype(BF16)
    cos = cos_ref[...]
    sin = sin_ref[...]
    n_dil = 0
    for j, (mode, scale) in enumerate(zip(SEG_MODES, SEG_SCALES)):
        acc = jnp.dot(xb, w_ref[:, j * SEG:(j + 1) * SEG], preferred_element_type=F32)
        if mode == "lat":
            outs.pop(0)[...] = acc
            continue
        if mode == "silu":
            outs.pop(0)[...] = (acc * jax.nn.sigmoid(acc)).astype(BF16)
            continue
        pattern_outs = [outs.pop(0) for _ in DILATED_PATTERNS]
        for h in range(B_HEADS):
            xh = acc[:, h * B_HDIM:(h + 1) * B_HDIM]
            if mode == "rope":
                xh = (xh * cos + pltpu.roll(xh, B_HDIM // 2, 1) * sin) * scale
            seg_ref[n_dil, h] = xh
            for o_ref, (_, dil) in zip(pattern_outs, DILATED_PATTERNS):
                if dil == 1:
                    o_ref[:, h * B_HDIM:(h + 1) * B_HDIM] = xh.astype(BF16)
                    continue
                for r in range(dil):
                    col = r * SEG + h * B_HDIM
                    o_ref[:, col:col + B_HDIM] = seg_ref[n_dil, h, pl.ds(r, tm // dil, stride=dil), :].astype(BF16)
        n_dil += 1


def _in_proj(x2, w_seg, cos_b, sin_b, s, tm):
    n = x2.shape[0]
    n_pos = s // tm
    row = lambda i: (i, 0)
    out_specs, out_shape = [], []
    for mode in SEG_MODES:
        if mode in ("lat", "silu"):
            out_specs.append(pl.BlockSpec((tm, SEG), row))
            out_shape.append(jax.ShapeDtypeStruct((n, SEG), F32 if mode == "lat" else BF16))
        else:
            out_specs += [pl.BlockSpec((tm // d, d * SEG), row) for _, d in DILATED_PATTERNS]
            out_shape += [jax.ShapeDtypeStruct((n // d, d * SEG), BF16) for _, d in DILATED_PATTERNS]
    flat = pl.pallas_call(
        _in_proj_kernel,
        grid=(n // tm,),
        in_specs=[
            pl.BlockSpec((tm, D_MODEL), row),
            pl.BlockSpec((D_MODEL, N_SEG * SEG), lambda i: (0, 0), pipeline_mode=pl.Buffered(1)),
            pl.BlockSpec((tm, B_HDIM), lambda i: (i % n_pos, 0)),
            pl.BlockSpec((tm, B_HDIM), lambda i: (i % n_pos, 0)),
        ],
        out_specs=out_specs,
        out_shape=out_shape,
        scratch_shapes=[pltpu.VMEM((N_DIL_OPERANDS, B_HEADS, tm, B_HDIM), F32)],
        compiler_params=_cparams(("parallel",)),
        name="in_proj",
    )(x2, w_seg, cos_b, sin_b)
    flat = list(flat)
    result = []
    for mode in SEG_MODES:
        if mode in ("lat", "silu"):
            result.append(flat.pop(0))
        else:
            result.append(tuple(flat.pop(0) for _ in DILATED_PATTERNS))
    return result


def _rms(x, g):
    return x * lax.rsqrt(jnp.mean(x * x, axis=-1, keepdims=True) + RMS_EPS) * g


def _mla_up_kernel(lat_ref, gq_ref, gkv_ref, wuqt_ref, wuk_ref, wuvt_ref,
                   cosk_ref, sink_ref, cost_ref, sint_ref, qt_ref, k_ref, vt_ref):
    lat = lat_ref[0]
    cqn = _rms(lat[:, :Q_LORA], gq_ref[...]).astype(BF16)
    ckvn = _rms(lat[:, Q_LORA:Q_LORA + KV_LORA], gkv_ref[...]).astype(BF16)
    kr = lat[:, Q_LORA + KV_LORA:Q_LORA + KV_LORA + LANES]
    lane = lax.broadcasted_iota(jnp.int32, kr.shape, 1)
    half = A_ROPE // 2
    rot = jnp.where(lane < half, pltpu.roll(kr, LANES - half, 1), pltpu.roll(kr, half, 1))
    kpe = (kr * cosk_ref[...] + rot * sink_ref[...]).astype(BF16)
    ct = cost_ref[...]
    st = sint_ref[...]
    for h in range(A_HEADS):
        qt = _nt_dot(wuqt_ref[h], cqn)
        x1 = qt[A_NOPE:A_NOPE + half]
        x2 = qt[A_NOPE + half:]
        qt_ref[0, h, 0:A_NOPE, :] = (qt[:A_NOPE] * A_SCALE).astype(BF16)
        qt_ref[0, h, A_NOPE:A_NOPE + half, :] = ((x1 * ct - x2 * st) * A_SCALE).astype(BF16)
        qt_ref[0, h, A_NOPE + half:, :] = ((x2 * ct + x1 * st) * A_SCALE).astype(BF16)
        k_ref[0, h, :, 0:A_NOPE] = jnp.dot(ckvn, wuk_ref[h], preferred_element_type=F32).astype(BF16)
        k_ref[0, h, :, A_NOPE:] = kpe[:, :A_ROPE]
        vt_ref[0, h, 0:A_VDIM, :] = _nt_dot(wuvt_ref[h], ckvn).astype(BF16)
        vt_ref[0, h, A_VDIM:, :] = jnp.ones((A_VROWS - A_VDIM, vt_ref.shape[3]), BF16)


def _mla_up(lat3, gq, gkv, wuqt, wuk, wuvt, cosk, sink, cost, sint, tm):
    b, s, _ = lat3.shape
    full = lambda *shape: pl.BlockSpec(shape, lambda bi, si: (0,) * len(shape))
    return pl.pallas_call(
        _mla_up_kernel,
        grid=(b, s // tm),
        in_specs=[
            pl.BlockSpec((1, tm, SEG), lambda bi, si: (bi, si, 0)),
            full(1, Q_LORA), full(1, KV_LORA),
            full(A_HEADS, A_QK, Q_LORA), full(A_HEADS, KV_LORA, A_NOPE), full(A_HEADS, A_VDIM, KV_LORA),
            pl.BlockSpec((tm, LANES), lambda bi, si: (si, 0)),
            pl.BlockSpec((tm, LANES), lambda bi, si: (si, 0)),
            pl.BlockSpec((A_ROPE // 2, tm), lambda bi, si: (0, si)),
            pl.BlockSpec((A_ROPE // 2, tm), lambda bi, si: (0, si)),
        ],
        out_specs=[
            pl.BlockSpec((1, A_HEADS, A_QK, tm), lambda bi, si: (bi, 0, 0, si)),
            pl.BlockSpec((1, A_HEADS, tm, A_QK), lambda bi, si: (bi, 0, si, 0)),
            pl.BlockSpec((1, A_HEADS, A_VROWS, tm), lambda bi, si: (bi, 0, 0, si)),
        ],
        out_shape=[
            jax.ShapeDtypeStruct((b, A_HEADS, A_QK, s), BF16),
            jax.ShapeDtypeStruct((b, A_HEADS, s, A_QK), BF16),
            jax.ShapeDtypeStruct((b, A_HEADS, A_VROWS, s), BF16),
        ],
        compiler_params=_cparams(("parallel", "parallel")),
        name="mla_up",
    )(lat3, gq, gkv, wuqt, wuk, wuvt, cosk, sink, cost, sint)


FLASH_BUFS = 2
FLASH_UNROLL = 4
FLASH_FLAGS = None
FLASH_HEADROOM = 64.0


def _mla_flash_kernel(qt_ref, k_ref, vt_ref, o_ref, s_ref, p_ref, acc_ref, *, bk, nk):
    bq = qt_ref.shape[3]

    def key_block(t):
        return k_ref[0, 0, pl.ds(pl.multiple_of(t * bk, bk), bk), :]

    def value_block(t):
        return vt_ref[0, 0, :, pl.ds(pl.multiple_of(t * bk, bk), bk)]

    def finish():
        acc = acc_ref[...]
        o_ref[0] = (acc[:A_VDIM] * (1.0 / acc[A_VDIM:A_VDIM + 1])).T

    def scores_exp(t, slot, carry):
        m_cur, m_prev, _, excess = carry
        st = jnp.dot(key_block(t), qt_ref[0, 0], preferred_element_type=F32)
        cmax = jnp.max(st, axis=0, keepdims=True)
        p_ref[slot] = jnp.exp2((st - m_cur).astype(BF16))
        return jnp.maximum(m_cur, cmax), m_cur, jnp.exp2(m_prev - m_cur), jnp.maximum(excess, cmax - m_cur)

    def values(t, slot, alpha):
        acc_ref[...] = alpha * acc_ref[...] + jnp.dot(value_block(t), p_ref[slot], preferred_element_type=F32)

    def step(t, t_slot, carry):
        alpha_t = carry[2]
        carry = scores_exp(t + 1, (t_slot + 1) % FLASH_BUFS, carry)
        values(t, t_slot, alpha_t)
        return carry

    s0 = jnp.dot(key_block(0), qt_ref[0, 0], preferred_element_type=F32)
    s_ref[...] = s0
    m0 = jnp.max(s0, axis=0, keepdims=True)
    p_ref[0] = jnp.exp2((s_ref[...] - m0).astype(BF16))
    acc_ref[...] = jnp.zeros(acc_ref.shape, F32)
    carry = (m0, m0, jnp.ones((1, bq), F32), jnp.zeros((1, bq), F32))

    def group(j, carry):
        for c in range(FLASH_UNROLL):
            carry = step(FLASH_UNROLL * j + c, c % FLASH_BUFS, carry)
        return carry

    n_groups = (nk - 1) // FLASH_UNROLL
    carry = lax.fori_loop(0, n_groups, group, carry)
    for t in range(FLASH_UNROLL * n_groups, nk - 1):
        carry = step(t, t % FLASH_BUFS, carry)
    values(nk - 1, (nk - 1) % FLASH_BUFS, carry[2])
    finish()

    @pl.when(jnp.max(carry[3]) > FLASH_HEADROOM)
    def _():
        acc_ref[...] = jnp.zeros(acc_ref.shape, F32)

        def body(t, m):
            st = jnp.dot(key_block(t), qt_ref[0, 0], preferred_element_type=F32)
            m_new = jnp.maximum(m, jnp.max(st, axis=0, keepdims=True))
            p = jnp.exp2((st - m_new).astype(BF16))
            acc_ref[...] = jnp.exp2(m - m_new) * acc_ref[...] + jnp.dot(value_block(t), p, preferred_element_type=F32)
            return m_new

        lax.fori_loop(0, nk, body, jnp.full((1, bq), NEG_BIG, F32))
        finish()


def _mla_flash(qt, k, vt, bq, bk):
    b, h, _, s = qt.shape
    nk = s // bk
    assert nk >= 2
    kern = functools.partial(_mla_flash_kernel, bk=bk, nk=nk)
    return pl.pallas_call(
        kern,
        grid=(b, h, s // bq),
        in_specs=[
            pl.BlockSpec((1, 1, A_QK, bq), lambda bi, hi, qi: (bi, hi, 0, qi)),
            pl.BlockSpec((1, 1, s, A_QK), lambda bi, hi, qi: (bi, hi, 0, 0)),
            pl.BlockSpec((1, 1, A_VROWS, s), lambda bi, hi, qi: (bi, hi, 0, 0)),
        ],
        out_specs=pl.BlockSpec((1, bq, A_VDIM), lambda bi, hi, qi: (bi, qi, hi)),
        out_shape=jax.ShapeDtypeStruct((b, s, A_WIDTH), F32),
        scratch_shapes=[pltpu.VMEM((bk, bq), F32), pltpu.VMEM((FLASH_BUFS, bk, bq), BF16),
                        pltpu.VMEM((A_VROWS, bq), F32)],
        compiler_params=_cparams(("parallel", "parallel", "arbitrary"), FLASH_FLAGS),
        name="mla_flash",
    )(qt, k, vt)


def _dilated_kernel(q_ref, kp_ref, kc_ref, kn_ref, vp_ref, vc_ref, vn_ref, o_ref, lse_ref, *, bq, nblk):
    i = pl.program_id(2)
    nkeys = DIL_SUBQ + 2 * HALF_WIN
    n_sub = bq // DIL_SUBQ
    a = lax.broadcasted_iota(jnp.int32, (DIL_SUBQ, nkeys), 0)
    c = lax.broadcasted_iota(jnp.int32, (DIL_SUBQ, nkeys), 1)
    band = (c >= a) & (c <= a + 2 * HALF_WIN)
    lane = lax.broadcasted_iota(jnp.int32, (DIL_SUBQ, LANES), 1)

    def window(p_ref, c_ref, n_ref, j, hs):
        lo = j * DIL_SUBQ - HALF_WIN
        hi = lo + nkeys
        parts = []
        if lo < 0:
            parts.append(p_ref[0, :, hs])
        parts.append(c_ref[0, max(lo, 0):min(hi, bq), hs])
        if hi > bq:
            parts.append(n_ref[0, :, hs])
        return jnp.concatenate(parts, axis=0) if len(parts) > 1 else parts[0]

    for j in range(n_sub):
        mask = band
        if j == 0:
            mask = mask & ((c >= HALF_WIN) | (i > 0))
        if j == n_sub - 1:
            mask = mask & ((c < DIL_SUBQ + HALF_WIN) | (i < nblk - 1))
        rows = slice(j * DIL_SUBQ, (j + 1) * DIL_SUBQ)
        lse_all = jnp.zeros((DIL_SUBQ, LANES), F32)
        for h in range(B_HEADS):
            hs = slice(h * B_HDIM, (h + 1) * B_HDIM)
            k = window(kp_ref, kc_ref, kn_ref, j, hs)
            v = window(vp_ref, vc_ref, vn_ref, j, hs)
            sc = jnp.where(mask, _nt_dot(q_ref[0, rows, hs], k), NEG_BIG)
            m = jnp.max(sc, axis=1, keepdims=True)
            p = jnp.exp2(sc - m)
            den = jnp.sum(p, axis=1, keepdims=True)
            num = jnp.dot(p.astype(BF16), v, preferred_element_type=F32)
            o_ref[0, rows, hs] = num * (1.0 / den)
            lse_all = jnp.where(lane == h, m + jnp.log2(den), lse_all)
        lse_ref[0, rows, :] = lse_all


def _dilated(qv, kv, vv, b, dil, bq):
    m_len = qv.shape[0] // b
    nblk = m_len // bq
    sub = bq // HALF_WIN
    n_half = m_len // HALF_WIN
    view = lambda t: t.reshape(b, m_len, dil * B_WIDTH)
    cur = pl.BlockSpec((1, bq, B_WIDTH), lambda bi, r, i: (bi, i, r))
    prev = pl.BlockSpec((1, HALF_WIN, B_WIDTH), lambda bi, r, i: (bi, jnp.maximum(i * sub - 1, 0), r))
    nxt = pl.BlockSpec((1, HALF_WIN, B_WIDTH), lambda bi, r, i: (bi, jnp.minimum((i + 1) * sub, n_half - 1), r))
    kern = functools.partial(_dilated_kernel, bq=bq, nblk=nblk)
    o, lse = pl.pallas_call(
        kern,
        grid=(b, dil, nblk),
        in_specs=[cur, prev, cur, nxt, prev, cur, nxt],
        out_specs=[cur, pl.BlockSpec((1, bq, LANES), lambda bi, r, i: (bi, i, r))],
        out_shape=[jax.ShapeDtypeStruct((b, m_len, dil * B_WIDTH), F32),
                   jax.ShapeDtypeStruct((b, m_len, dil * LANES), F32)],
        compiler_params=_cparams(("parallel", "parallel", "parallel")),
        name=f"dilated_d{dil}",
    )(view(qv), view(kv), view(kv), view(kv), view(vv), view(vv), view(vv))
    return o.reshape(b * m_len, dil * B_WIDTH), lse.reshape(b * m_len, dil * LANES)


def _out_proj_kernel(oa_ref, o1_ref, o2_ref, o3_ref, l1_ref, l2_ref, l3_ref, ga_ref, gb_ref, x_ref,
                     w_ref, goa_ref, gob_ref, lng_ref, lnb_ref, y_ref, on_ref, ln_ref):
    za = (_rms(oa_ref[...], goa_ref[...]) * ga_ref[...].astype(F32)).astype(BF16)
    tm = oa_ref.shape[0]
    for i, (o_ref, l_ref, (_, dil)) in enumerate(zip((o1_ref, o2_ref, o3_ref), (l1_ref, l2_ref, l3_ref),
                                                     DILATED_PATTERNS)):
        for r in range(dil):
            rows = pl.ds(r, tm // dil, stride=dil) if dil > 1 else slice(None)
            ln_ref[i, rows, :] = l_ref[:, r * LANES:(r + 1) * LANES]
            for h in range(B_HEADS):
                col = r * B_WIDTH + h * B_HDIM
                on_ref[i, h, rows, :] = o_ref[:, col:col + B_HDIM]
    l1, l2, l3 = ln_ref[0], ln_ref[1], ln_ref[2]
    mx = jnp.maximum(jnp.maximum(l1, l2), l3)
    e1, e2, e3 = jnp.exp2(l1 - mx), jnp.exp2(l2 - mx), jnp.exp2(l3 - mx)
    inv = 1.0 / (e1 + e2 + e3)
    w1, w2, w3 = e1 * inv, e2 * inv, e3 * inv
    parts = []
    for h in range(B_HEADS):
        parts.append(w1[:, h:h + 1] * on_ref[0, h] + w2[:, h:h + 1] * on_ref[1, h] + w3[:, h:h + 1] * on_ref[2, h])
    ob = jnp.concatenate(parts, axis=1)
    zb = (_rms(ob, gob_ref[...]) * gb_ref[...].astype(F32)).astype(BF16)
    out = (jnp.dot(za, w_ref[:A_WIDTH, :], preferred_element_type=F32)
           + jnp.dot(zb, w_ref[A_WIDTH:, :], preferred_element_type=F32))
    r = DEEPNORM_ALPHA * x_ref[...] + out
    mu = jnp.mean(r, axis=-1, keepdims=True)
    d = r - mu
    var = jnp.mean(d * d, axis=-1, keepdims=True)
    y_ref[...] = d * lax.rsqrt(var + LN_EPS) * lng_ref[...] + lnb_ref[...]


def _out_proj(oa, obs, lses, ga, gb, x2, w_out, goa, gob, lng, lnb, tm):
    n = x2.shape[0]
    row = lambda w: pl.BlockSpec((tm, w), lambda i: (i, 0))
    view = lambda w: [pl.BlockSpec((tm // d, d * w), lambda i: (i, 0)) for _, d in DILATED_PATTERNS]
    full = lambda *shape: pl.BlockSpec(shape, lambda i: (0,) * len(shape))
    n_pat = len(DILATED_PATTERNS)
    return pl.pallas_call(
        _out_proj_kernel,
        grid=(n // tm,),
        in_specs=[row(A_WIDTH)] + view(B_WIDTH) + view(LANES) + [row(A_WIDTH), row(B_WIDTH), row(D_MODEL),
                  full(A_WIDTH + B_WIDTH, D_MODEL), full(1, A_WIDTH), full(1, B_WIDTH),
                  full(1, D_MODEL), full(1, D_MODEL)],
        out_specs=row(D_MODEL),
        out_shape=jax.ShapeDtypeStruct((n, D_MODEL), F32),
        scratch_shapes=[pltpu.VMEM((n_pat, B_HEADS, tm, B_HDIM), F32), pltpu.VMEM((n_pat, tm, LANES), F32)],
        compiler_params=_cparams(("parallel",)),
        name="out_proj",
    )(oa, *obs, *lses, ga, gb, x2, w_out, goa, gob, lng, lnb)


def _rope_tables(s):
    pos = jnp.arange(s, dtype=F32)

    def cos_sin(d):
        half = d // 2
        inv = ROPE_THETA ** (-jnp.arange(half, dtype=F32) * 2.0 / d)
        ang = pos[:, None] * inv[None, :]
        return jnp.cos(ang), jnp.sin(ang)

    cb, sb = cos_sin(B_HDIM)
    cos_b = jnp.concatenate([cb, cb], axis=1)
    sin_b = jnp.concatenate([-sb, sb], axis=1)
    ca, sa = cos_sin(A_ROPE)
    zeros = jnp.zeros((s, LANES - A_ROPE), F32)
    cos_k = jnp.concatenate([ca, ca, zeros], axis=1)
    sin_k = jnp.concatenate([-sa, sa, zeros], axis=1)
    return cos_b, sin_b, cos_k, sin_k, ca.T, sa.T


def _prep_weights(w_in, g_qn, w_uq, g_kvn, w_ukv, g_oa, g_ob, w_out, ln_g, ln_b):
    n_lat = Q_LORA + KV_LORA + A_ROPE
    w_seg = jnp.concatenate([w_in[:, :n_lat], jnp.zeros((D_MODEL, LAT_PAD), w_in.dtype), w_in[:, n_lat:]],
                            axis=1).astype(BF16)
    wuqt = w_uq.reshape(Q_LORA, A_HEADS, A_QK).transpose(1, 2, 0).astype(BF16)
    wukv = w_ukv.reshape(KV_LORA, A_HEADS, A_NOPE + A_VDIM)
    wuk = wukv[:, :, :A_NOPE].transpose(1, 0, 2).astype(BF16)
    wuvt = wukv[:, :, A_NOPE:].transpose(1, 2, 0).astype(BF16)
    return dict(w_seg=w_seg, gq=g_qn[None, :], gkv=g_kvn[None, :], wuqt=wuqt, wuk=wuk, wuvt=wuvt,
                goa=g_oa[None, :], gob=g_ob[None, :], w_out=w_out.astype(BF16),
                lng=ln_g[None, :], lnb=ln_b[None, :])


def _tiles(s):
    pick = lambda pref, n: pref if n % pref == 0 else n
    return dict(tm_in=pick(256, s), tm_up=pick(512, s), bq=pick(1024, s), bk=pick(512, s), tm_out=pick(256, s))


def _encoder_layer(x, p):
    b, s, _ = x.shape
    longest = max(w // 2 for w, _ in DILATED_PATTERNS)
    assert s % longest == 0, "sequence length must be a multiple of the largest dilated chunk"
    t = _tiles(s)
    cos_b, sin_b, cos_k, sin_k, cos_t, sin_t = _rope_tables(s)
    x2 = x.reshape(b * s, D_MODEL)
    lat, qvs, kvs, vvs, ga, gb = _in_proj(x2, p["w_seg"], cos_b, sin_b, s, t["tm_in"])
    qt, k, vt = _mla_up(lat.reshape(b, s, SEG), p["gq"], p["gkv"], p["wuqt"], p["wuk"], p["wuvt"],
                        cos_k, sin_k, cos_t, sin_t, t["tm_up"])
    oa = _mla_flash(qt, k, vt, t["bq"], t["bk"]).reshape(b * s, A_WIDTH)
    obs, lses = [], []
    for i, (window, dil) in enumerate(DILATED_PATTERNS):
        assert window // (2 * dil) == HALF_WIN
        o, lse = _dilated(qvs[i], kvs[i], vvs[i], b, dil, min(DIL_BQ, s // dil))
        obs.append(o)
        lses.append(lse)
    y = _out_proj(oa, obs, lses, ga, gb, x2, p["w_out"], p["goa"], p["gob"], p["lng"], p["lnb"], t["tm_out"])
    return y.reshape(b, s, D_MODEL)


def kernel(x_prompt, x_sample, w_in, g_qn, w_uq, g_kvn, w_ukv, g_oa, g_ob, w_out, ln_g, ln_b):
    def trunk(x):
        for l in range(DEPTH):
            p = _prep_weights(w_in[l], g_qn[l], w_uq[l], g_kvn[l], w_ukv[l], g_oa[l], g_ob[l],
                              w_out[l], ln_g[l], ln_b[l])
            x = _encoder_layer(x, p)
        return x

    return (trunk(x_prompt), trunk(x_sample))
```

```python
import functools

import jax
import jax.numpy as jnp
from jax import lax
from jax.experimental import pallas as pl
from jax.experimental.pallas import tpu as pltpu

F32 = jnp.float32
BF16 = jnp.bfloat16

D_MODEL = 2048
DEPTH = 1
A_HEADS = 8
A_NOPE = 128
A_ROPE = 64
A_VDIM = 128
A_QK = A_NOPE + A_ROPE
A_WIDTH = A_HEADS * A_VDIM
A_VROWS = A_VDIM + 16
Q_LORA = 512
KV_LORA = 256
B_HEADS = 8
B_HDIM = 128
B_WIDTH = B_HEADS * B_HDIM
DILATED_PATTERNS = ((128, 1), (512, 4), (2048, 16))
ROPE_THETA = 10000.0
RMS_EPS = 1e-6
LN_EPS = 1e-5
NEG_BIG = -1e30
DEEPNORM_ALPHA = (2 * DEPTH) ** 0.25
LOG2E = 1.4426950408889634
A_SCALE = A_QK ** -0.5 * LOG2E
B_SCALE = B_HDIM ** -0.5 * LOG2E

LANES = 128
SEG = 1024
N_SEG = 6
LAT_PAD = SEG - (Q_LORA + KV_LORA + A_ROPE)
HALF_WIN = 64
DIL_SUBQ = 128
DIL_BQ = 512
VMEM_LIMIT = 56 * 1024 * 1024


def _cparams(semantics, flags=None):
    return pltpu.CompilerParams(dimension_semantics=semantics, vmem_limit_bytes=VMEM_LIMIT, flags=flags)


def _nt_dot(a, b):
    return lax.dot_general(a, b, (((1,), (1,)), ((), ())), preferred_element_type=F32)


SEG_MODES = ("lat", "rope", "rope", "plain", "silu", "silu")
SEG_SCALES = (1.0, B_SCALE, 1.0, 1.0, 1.0, 1.0)
N_DIL_OPERANDS = sum(m in ("rope", "plain") for m in SEG_MODES)


def _in_proj_kernel(x_ref, wl_ref, w_ref, cos_ref, sin_ref, *refs):
    outs, seg_ref = list(refs[:-1]), refs[-1]
    tm = x_ref.shape[0]
    xb = x_ref[...].astype(BF16)
    cos = cos_ref[...]
    sin = sin_ref[...]
    n_dil = 0
    for j, (mode, scale) in enumerate(zip(SEG_MODES, SEG_SCALES)):
        w = wl_ref[...] if j == 0 else w_ref[:, (j - 1) * SEG:j * SEG]
        acc = jnp.dot(xb, w, preferred_element_type=F32)
        if mode == "lat":
            outs.pop(0)[...] = acc
            continue
        if mode == "silu":
            outs.pop(0)[...] = (acc * jax.nn.sigmoid(acc)).astype(BF16)
            continue
        pattern_outs = [outs.pop(0) for _ in DILATED_PATTERNS]
        for h in range(B_HEADS):
            xh = acc[:, h * B_HDIM:(h + 1) * B_HDIM]
            if mode == "rope":
                xh = (xh * cos + pltpu.roll(xh, B_HDIM // 2, 1) * sin) * scale
            seg_ref[n_dil, h] = xh
            for o_ref, (_, dil) in zip(pattern_outs, DILATED_PATTERNS):
                if dil == 1:
                    o_ref[:, h * B_HDIM:(h + 1) * B_HDIM] = xh.astype(BF16)
                    continue
                for r in range(dil):
                    col = r * SEG + h * B_HDIM
                    o_ref[:, col:col + B_HDIM] = seg_ref[n_dil, h, pl.ds(r, tm // dil, stride=dil), :].astype(BF16)
        n_dil += 1


def _in_proj(x2, w_lat, w_rest, cos_b, sin_b, s, tm):
    n = x2.shape[0]
    n_pos = s // tm
    row = lambda i: (i, 0)
    out_specs, out_shape = [], []
    for mode in SEG_MODES:
        if mode in ("lat", "silu"):
            out_specs.append(pl.BlockSpec((tm, SEG), row))
            out_shape.append(jax.ShapeDtypeStruct((n, SEG), F32 if mode == "lat" else BF16))
        else:
            out_specs += [pl.BlockSpec((tm // d, d * SEG), row) for _, d in DILATED_PATTERNS]
            out_shape += [jax.ShapeDtypeStruct((n // d, d * SEG), BF16) for _, d in DILATED_PATTERNS]
    flat = pl.pallas_call(
        _in_proj_kernel,
        grid=(n // tm,),
        in_specs=[
            pl.BlockSpec((tm, D_MODEL), row),
            pl.BlockSpec((D_MODEL, SEG), lambda i: (0, 0), pipeline_mode=pl.Buffered(1)),
            pl.BlockSpec((D_MODEL, (N_SEG - 1) * SEG), lambda i: (0, 0), pipeline_mode=pl.Buffered(1)),
            pl.BlockSpec((tm, B_HDIM), lambda i: (i % n_pos, 0)),
            pl.BlockSpec((tm, B_HDIM), lambda i: (i % n_pos, 0)),
        ],
        out_specs=out_specs,
        out_shape=out_shape,
        scratch_shapes=[pltpu.VMEM((N_DIL_OPERANDS, B_HEADS, tm, B_HDIM), F32)],
        compiler_params=_cparams(("parallel",)),
        name="in_proj",
    )(x2, w_lat, w_rest, cos_b, sin_b)
    flat = list(flat)
    result = []
    for mode in SEG_MODES:
        if mode in ("lat", "silu"):
            result.append(flat.pop(0))
        else:
            result.append(tuple(flat.pop(0) for _ in DILATED_PATTERNS))
    return result


def _rms(x, g):
    return x * lax.rsqrt(jnp.mean(x * x, axis=-1, keepdims=True) + RMS_EPS) * g


def _mla_up_kernel(lat_ref, gq_ref, gkv_ref, wuqt_ref, wuk_ref, wuvt_ref,
                   cosk_ref, sink_ref, cost_ref, sint_ref, qt_ref, k_ref, vt_ref):
    lat = lat_ref[0]
    cqn = _rms(lat[:, :Q_LORA], gq_ref[...]).astype(BF16)
    ckvn = _rms(lat[:, Q_LORA:Q_LORA + KV_LORA], gkv_ref[...]).astype(BF16)
    kr = lat[:, Q_LORA + KV_LORA:Q_LORA + KV_LORA + LANES]
    lane = lax.broadcasted_iota(jnp.int32, kr.shape, 1)
    half = A_ROPE // 2
    rot = jnp.where(lane < half, pltpu.roll(kr, LANES - half, 1), pltpu.roll(kr, half, 1))
    kpe = (kr * cosk_ref[...] + rot * sink_ref[...]).astype(BF16)
    ct = cost_ref[...]
    st = sint_ref[...]
    for h in range(A_HEADS):
        qt = _nt_dot(wuqt_ref[h], cqn)
        x1 = qt[A_NOPE:A_NOPE + half]
        x2 = qt[A_NOPE + half:]
        qt_ref[0, h, 0:A_NOPE, :] = (qt[:A_NOPE] * A_SCALE).astype(BF16)
        qt_ref[0, h, A_NOPE:A_NOPE + half, :] = ((x1 * ct - x2 * st) * A_SCALE).astype(BF16)
        qt_ref[0, h, A_NOPE + half:, :] = ((x2 * ct + x1 * st) * A_SCALE).astype(BF16)
        k_ref[0, h, :, 0:A_NOPE] = jnp.dot(ckvn, wuk_ref[h], preferred_element_type=F32).astype(BF16)
        k_ref[0, h, :, A_NOPE:] = kpe[:, :A_ROPE]
        vt_ref[0, h, 0:A_VDIM, :] = _nt_dot(wuvt_ref[h], ckvn).astype(BF16)
        vt_ref[0, h, A_VDIM:, :] = jnp.ones((A_VROWS - A_VDIM, vt_ref.shape[3]), BF16)


def _mla_up(lat3, gq, gkv, wuqt, wuk, wuvt, cosk, sink, cost, sint, tm):
    b, s, _ = lat3.shape
    full = lambda *shape: pl.BlockSpec(shape, lambda bi, si: (0,) * len(shape))
    return pl.pallas_call(
        _mla_up_kernel,
        grid=(b, s // tm),
        in_specs=[
            pl.BlockSpec((1, tm, SEG), lambda bi, si: (bi, si, 0)),
            full(1, Q_LORA), full(1, KV_LORA),
            full(A_HEADS, A_QK, Q_LORA), full(A_HEADS, KV_LORA, A_NOPE), full(A_HEADS, A_VDIM, KV_LORA),
            pl.BlockSpec((tm, LANES), lambda bi, si: (si, 0)),
            pl.BlockSpec((tm, LANES), lambda bi, si: (si, 0)),
            pl.BlockSpec((A_ROPE // 2, tm), lambda bi, si: (0, si)),
            pl.BlockSpec((A_ROPE // 2, tm), lambda bi, si: (0, si)),
        ],
        out_specs=[
            pl.BlockSpec((1, A_HEADS, A_QK, tm), lambda bi, si: (bi, 0, 0, si)),
            pl.BlockSpec((1, A_HEADS, tm, A_QK), lambda bi, si: (bi, 0, si, 0)),
            pl.BlockSpec((1, A_HEADS, A_VROWS, tm), lambda bi, si: (bi, 0, 0, si)),
        ],
        out_shape=[
            jax.ShapeDtypeStruct((b, A_HEADS, A_QK, s), BF16),
            jax.ShapeDtypeStruct((b, A_HEADS, s, A_QK), BF16),
            jax.ShapeDtypeStruct((b, A_HEADS, A_VROWS, s), BF16),
        ],
        compiler_params=_cparams(("parallel", "parallel")),
        name="mla_up",
    )(lat3, gq, gkv, wuqt, wuk, wuvt, cosk, sink, cost, sint)


FLASH_BUFS = 2
FLASH_UNROLL = 4
FLASH_FLAGS = None
FLASH_HEADROOM = 64.0


def _mla_flash_kernel(qt_ref, k_ref, vt_ref, o_ref, s_ref, p_ref, acc_ref, *, bk, nk):
    bq = qt_ref.shape[3]

    def key_block(t):
        return k_ref[0, 0, pl.ds(pl.multiple_of(t * bk, bk), bk), :]

    def value_block(t):
        return vt_ref[0, 0, :, pl.ds(pl.multiple_of(t * bk, bk), bk)]

    def finish():
        acc = acc_ref[...]
        o_ref[0] = (acc[:A_VDIM] * (1.0 / acc[A_VDIM:A_VDIM + 1])).T

    def scores_exp(t, slot, carry):
        m_cur, m_prev, _, excess = carry
        st = jnp.dot(key_block(t), qt_ref[0, 0], preferred_element_type=F32)
        cmax = jnp.max(st, axis=0, keepdims=True)
        p_ref[slot] = jnp.exp2((st - m_cur).astype(BF16))
        return jnp.maximum(m_cur, cmax), m_cur, jnp.exp2(m_prev - m_cur), jnp.maximum(excess, cmax - m_cur)

    def values(t, slot, alpha):
        acc_ref[...] = alpha * acc_ref[...] + jnp.dot(value_block(t), p_ref[slot], preferred_element_type=F32)

    def step(t, t_slot, carry):
        alpha_t = carry[2]
        carry = scores_exp(t + 1, (t_slot + 1) % FLASH_BUFS, carry)
        values(t, t_slot, alpha_t)
        return carry

    s0 = jnp.dot(key_block(0), qt_ref[0, 0], preferred_element_type=F32)
    s_ref[...] = s0
    m0 = jnp.max(s0, axis=0, keepdims=True)
    p_ref[0] = jnp.exp2((s_ref[...] - m0).astype(BF16))
    acc_ref[...] = jnp.zeros(acc_ref.shape, F32)
    carry = (m0, m0, jnp.ones((1, bq), F32), jnp.zeros((1, bq), F32))

    def group(j, carry):
        for c in range(FLASH_UNROLL):
            carry = step(FLASH_UNROLL * j + c, c % FLASH_BUFS, carry)
        return carry

    n_groups = (nk - 1) // FLASH_UNROLL
    carry = lax.fori_loop(0, n_groups, group, carry)
    for t in range(FLASH_UNROLL * n_groups, nk - 1):
        carry = step(t, t % FLASH_BUFS, carry)
    values(nk - 1, (nk - 1) % FLASH_BUFS, carry[2])
    finish()

    @pl.when(jnp.max(carry[3]) > FLASH_HEADROOM)
    def _():
        acc_ref[...] = jnp.zeros(acc_ref.shape, F32)

        def body(t, m):
            st = jnp.dot(key_block(t), qt_ref[0, 0], preferred_element_type=F32)
            m_new = jnp.maximum(m, jnp.max(st, axis=0, keepdims=True))
            p = jnp.exp2((st - m_new).astype(BF16))
            acc_ref[...] = jnp.exp2(m - m_new) * acc_ref[...] + jnp.dot(value_block(t), p, preferred_element_type=F32)
            return m_new

        lax.fori_loop(0, nk, body, jnp.full((1, bq), NEG_BIG, F32))
        finish()


def _mla_flash(qt, k, vt, bq, bk):
    b, h, _, s = qt.shape
    nk = s // bk
    assert nk >= 2
    kern = functools.partial(_mla_flash_kernel, bk=bk, nk=nk)
    return pl.pallas_call(
        kern,
        grid=(b, h, s // bq),
        in_specs=[
            pl.BlockSpec((1, 1, A_QK, bq), lambda bi, hi, qi: (bi, hi, 0, qi)),
            pl.BlockSpec((1, 1, s, A_QK), lambda bi, hi, qi: (bi, hi, 0, 0)),
            pl.BlockSpec((1, 1, A_VROWS, s), lambda bi, hi, qi: (bi, hi, 0, 0)),
        ],
        out_specs=pl.BlockSpec((1, bq, A_VDIM), lambda bi, hi, qi: (bi, qi, hi)),
        out_shape=jax.ShapeDtypeStruct((b, s, A_WIDTH), F32),
        scratch_shapes=[pltpu.VMEM((bk, bq), F32), pltpu.VMEM((FLASH_BUFS, bk, bq), BF16),
                        pltpu.VMEM((A_VROWS, bq), F32)],
        compiler_params=_cparams(("parallel", "parallel", "arbitrary"), FLASH_FLAGS),
        name="mla_flash",
    )(qt, k, vt)


def _dilated_kernel(q_ref, kp_ref, kc_ref, kn_ref, vp_ref, vc_ref, vn_ref, o_ref, lse_ref, *, bq, nblk):
    i = pl.program_id(2)
    nkeys = DIL_SUBQ + 2 * HALF_WIN
    n_sub = bq // DIL_SUBQ
    a = lax.broadcasted_iota(jnp.int32, (DIL_SUBQ, nkeys), 0)
    c = lax.broadcasted_iota(jnp.int32, (DIL_SUBQ, nkeys), 1)
    band = (c >= a) & (c <= a + 2 * HALF_WIN)
    lane = lax.broadcasted_iota(jnp.int32, (DIL_SUBQ, LANES), 1)

    def window(p_ref, c_ref, n_ref, j, hs):
        lo = j * DIL_SUBQ - HALF_WIN
        hi = lo + nkeys
        parts = []
        if lo < 0:
            parts.append(p_ref[0, :, hs])
        parts.append(c_ref[0, max(lo, 0):min(hi, bq), hs])
        if hi > bq:
            parts.append(n_ref[0, :, hs])
        return jnp.concatenate(parts, axis=0) if len(parts) > 1 else parts[0]

    for j in range(n_sub):
        mask = band
        if j == 0:
            mask = mask & ((c >= HALF_WIN) | (i > 0))
        if j == n_sub - 1:
            mask = mask & ((c < DIL_SUBQ + HALF_WIN) | (i < nblk - 1))
        rows = slice(j * DIL_SUBQ, (j + 1) * DIL_SUBQ)
        lse_all = jnp.zeros((DIL_SUBQ, LANES), F32)
        for h in range(B_HEADS):
            hs = slice(h * B_HDIM, (h + 1) * B_HDIM)
            k = window(kp_ref, kc_ref, kn_ref, j, hs)
            v = window(vp_ref, vc_ref, vn_ref, j, hs)
            sc = jnp.where(mask, _nt_dot(q_ref[0, rows, hs], k), NEG_BIG)
            m = jnp.max(sc, axis=1, keepdims=True)
            p = jnp.exp2(sc - m).astype(BF16)
            both = jnp.dot(p, jnp.concatenate([v, jnp.ones_like(v)], axis=1), preferred_element_type=F32)
            den = both[:, B_HDIM:]
            o_ref[0, rows, hs] = both[:, :B_HDIM] * (1.0 / den)
            lse_all = jnp.where(lane == h, m + jnp.log2(den), lse_all)
        lse_ref[0, rows, :] = lse_all


def _dilated(qv, kv, vv, b, dil, bq):
    m_len = qv.shape[0] // b
    nblk = m_len // bq
    sub = bq // HALF_WIN
    n_half = m_len // HALF_WIN
    view = lambda t: t.reshape(b, m_len, dil * B_WIDTH)
    cur = pl.BlockSpec((1, bq, B_WIDTH), lambda bi, r, i: (bi, i, r))
    prev = pl.BlockSpec((1, HALF_WIN, B_WIDTH), lambda bi, r, i: (bi, jnp.maximum(i * sub - 1, 0), r))
    nxt = pl.BlockSpec((1, HALF_WIN, B_WIDTH), lambda bi, r, i: (bi, jnp.minimum((i + 1) * sub, n_half - 1), r))
    kern = functools.partial(_dilated_kernel, bq=bq, nblk=nblk)
    o, lse = pl.pallas_call(
        kern,
        grid=(b, dil, nblk),
        in_specs=[cur, prev, cur, nxt, prev, cur, nxt],
        out_specs=[cur, pl.BlockSpec((1, bq, LANES), lambda bi, r, i: (bi, i, r))],
        out_shape=[jax.ShapeDtypeStruct((b, m_len, dil * B_WIDTH), F32),
                   jax.ShapeDtypeStruct((b, m_len, dil * LANES), F32)],
        compiler_params=_cparams(("parallel", "parallel", "parallel")),
        name=f"dilated_d{dil}",
    )(view(qv), view(kv), view(kv), view(kv), view(vv), view(vv), view(vv))
    return o.reshape(b * m_len, dil * B_WIDTH), lse.reshape(b * m_len, dil * LANES)


OUT_GROUPS = 2


def _out_proj_kernel(oa_ref, o1_ref, o2_ref, o3_ref, l1_ref, l2_ref, l3_ref, ga_ref, gb_ref, x_ref,
                     w_ref, goa_ref, gob_ref, lng_ref, lnb_ref, y_ref, on_ref, ln_ref):
    tm = oa_ref.shape[0]
    for i, (o_ref, l_ref, (_, dil)) in enumerate(zip((o1_ref, o2_ref, o3_ref), (l1_ref, l2_ref, l3_ref),
                                                     DILATED_PATTERNS)):
        for r in range(dil):
            rows = pl.ds(r, tm // dil, stride=dil) if dil > 1 else slice(None)
            ln_ref[i, rows, :] = l_ref[:, r * LANES:(r + 1) * LANES]
            for h in range(B_HEADS):
                col = r * B_WIDTH + h * B_HDIM
                on_ref[i, h, rows, :] = o_ref[:, col:col + B_HDIM]
    for g in range(OUT_GROUPS):
        rows = slice(g * tm // OUT_GROUPS, (g + 1) * tm // OUT_GROUPS)
        za = (_rms(oa_ref[rows, :], goa_ref[...]) * ga_ref[rows, :].astype(F32)).astype(BF16)
        l1, l2, l3 = ln_ref[0, rows, :], ln_ref[1, rows, :], ln_ref[2, rows, :]
        mx = jnp.maximum(jnp.maximum(l1, l2), l3)
        e1, e2, e3 = jnp.exp2(l1 - mx), jnp.exp2(l2 - mx), jnp.exp2(l3 - mx)
        inv = 1.0 / (e1 + e2 + e3)
        w1, w2, w3 = e1 * inv, e2 * inv, e3 * inv
        parts = []
        for h in range(B_HEADS):
            parts.append(w1[:, h:h + 1] * on_ref[0, h, rows, :] + w2[:, h:h + 1] * on_ref[1, h, rows, :]
                         + w3[:, h:h + 1] * on_ref[2, h, rows, :])
        ob = jnp.concatenate(parts, axis=1)
        zb = (_rms(ob, gob_ref[...]) * gb_ref[rows, :].astype(F32)).astype(BF16)
        out = (jnp.dot(za, w_ref[:A_WIDTH, :], preferred_element_type=F32)
               + jnp.dot(zb, w_ref[A_WIDTH:, :], preferred_element_type=F32))
        r = DEEPNORM_ALPHA * x_ref[rows, :] + out
        mu = jnp.mean(r, axis=-1, keepdims=True)
        d = r - mu
        var = jnp.mean(d * d, axis=-1, keepdims=True)
        y_ref[rows, :] = d * lax.rsqrt(var + LN_EPS) * lng_ref[...] + lnb_ref[...]


def _out_proj(oa, obs, lses, ga, gb, x2, w_out, goa, gob, lng, lnb, tm):
    n = x2.shape[0]
    row = lambda w: pl.BlockSpec((tm, w), lambda i: (i, 0))
    view = lambda w: [pl.BlockSpec((tm // d, d * w), lambda i: (i, 0)) for _, d in DILATED_PATTERNS]
    full = lambda *shape: pl.BlockSpec(shape, lambda i: (0,) * len(shape))
    n_pat = len(DILATED_PATTERNS)
    return pl.pallas_call(
        _out_proj_kernel,
        grid=(n // tm,),
        in_specs=[row(A_WIDTH)] + view(B_WIDTH) + view(LANES) + [row(A_WIDTH), row(B_WIDTH), row(D_MODEL),
                  full(A_WIDTH + B_WIDTH, D_MODEL), full(1, A_WIDTH), full(1, B_WIDTH),
                  full(1, D_MODEL), full(1, D_MODEL)],
        out_specs=row(D_MODEL),
        out_shape=jax.ShapeDtypeStruct((n, D_MODEL), F32),
        scratch_shapes=[pltpu.VMEM((n_pat, B_HEADS, tm, B_HDIM), F32), pltpu.VMEM((n_pat, tm, LANES), F32)],
        compiler_params=_cparams(("parallel",)),
        name="out_proj",
    )(oa, *obs, *lses, ga, gb, x2, w_out, goa, gob, lng, lnb)


def _rope_tables(s):
    pos = jnp.arange(s, dtype=F32)

    def cos_sin(d):
        half = d // 2
        inv = ROPE_THETA ** (-jnp.arange(half, dtype=F32) * 2.0 / d)
        ang = pos[:, None] * inv[None, :]
        return jnp.cos(ang), jnp.sin(ang)

    cb, sb = cos_sin(B_HDIM)
    cos_b = jnp.concatenate([cb, cb], axis=1)
    sin_b = jnp.concatenate([-sb, sb], axis=1)
    ca, sa = cos_sin(A_ROPE)
    zeros = jnp.zeros((s, LANES - A_ROPE), F32)
    cos_k = jnp.concatenate([ca, ca, zeros], axis=1)
    sin_k = jnp.concatenate([-sa, sa, zeros], axis=1)
    return cos_b, sin_b, cos_k, sin_k, ca.T, sa.T


def _prep_weights(w_in, g_qn, w_uq, g_kvn, w_ukv, g_oa, g_ob, w_out, ln_g, ln_b):
    n_lat = Q_LORA + KV_LORA + A_ROPE
    w_lat = jnp.pad(w_in[:, :n_lat], ((0, 0), (0, LAT_PAD))).astype(BF16)
    w_rest = w_in[:, n_lat:].astype(BF16)
    wuqt = w_uq.reshape(Q_LORA, A_HEADS, A_QK).transpose(1, 2, 0).astype(BF16)
    wukv = w_ukv.reshape(KV_LORA, A_HEADS, A_NOPE + A_VDIM)
    wuk = wukv[:, :, :A_NOPE].transpose(1, 0, 2).astype(BF16)
    wuvt = wukv[:, :, A_NOPE:].transpose(1, 2, 0).astype(BF16)
    return dict(w_lat=w_lat, w_rest=w_rest, gq=g_qn[None, :], gkv=g_kvn[None, :], wuqt=wuqt, wuk=wuk, wuvt=wuvt,
                goa=g_oa[None, :], gob=g_ob[None, :], w_out=w_out.astype(BF16),
                lng=ln_g[None, :], lnb=ln_b[None, :])


def _tiles(s):
    pick = lambda pref, n: pref if n % pref == 0 else n
    return dict(tm_in=pick(256, s), tm_up=pick(512, s), bq=pick(1024, s), bk=pick(512, s), tm_out=pick(256, s))


def _encoder_layer(x, p):
    b, s, _ = x.shape
    longest = max(w // 2 for w, _ in DILATED_PATTERNS)
    assert s % longest == 0, "sequence length must be a multiple of the largest dilated chunk"
    t = _tiles(s)
    cos_b, sin_b, cos_k, sin_k, cos_t, sin_t = _rope_tables(s)
    x2 = x.reshape(b * s, D_MODEL)
    lat, qvs, kvs, vvs, ga, gb = _in_proj(x2, p["w_lat"], p["w_rest"], cos_b, sin_b, s, t["tm_in"])
    qt, k, vt = _mla_up(lat.reshape(b, s, SEG), p["gq"], p["gkv"], p["wuqt"], p["wuk"], p["wuvt"],
                        cos_k, sin_k, cos_t, sin_t, t["tm_up"])
    oa = _mla_flash(qt, k, vt, t["bq"], t["bk"]).reshape(b * s, A_WIDTH)
    obs, lses = [], []
    for i, (window, dil) in enumerate(DILATED_PATTERNS):
        assert window // (2 * dil) == HALF_WIN
        o, lse = _dilated(qvs[i], kvs[i], vvs[i], b, dil, min(DIL_BQ, s // dil))
        obs.append(o)
        lses.append(lse)
    y = _out_proj(oa, obs, lses, ga, gb, x2, p["w_out"], p["goa"], p["gob"], p["lng"], p["lnb"], t["tm_out"])
    return y.reshape(b, s, D_MODEL)


def kernel(x_prompt, x_sample, w_in, g_qn, w_uq, g_kvn, w_ukv, g_oa, g_ob, w_out, ln_g, ln_b):
    layers = [_prep_weights(w_in[l], g_qn[l], w_uq[l], g_kvn[l], w_ukv[l], g_oa[l], g_ob[l],
                            w_out[l], ln_g[l], ln_b[l]) for l in range(DEPTH)]

    def trunk(x):
        for p in layers:
            x = _encoder_layer(x, p)
        return x

    return (trunk(x_prompt), trunk(x_sample))
```

```python
import functools

import jax
import jax.numpy as jnp
from jax import lax
from jax.experimental import pallas as pl
from jax.experimental.pallas import tpu as pltpu

F32 = jnp.float32
BF16 = jnp.bfloat16

D_MODEL = 2048
DEPTH = 1
A_HEADS = 8
A_NOPE = 128
A_ROPE = 64
A_VDIM = 128
A_QK = A_NOPE + A_ROPE
A_WIDTH = A_HEADS * A_VDIM
A_VROWS = A_VDIM + 16
Q_LORA = 512
KV_LORA = 256
B_HEADS = 8
B_HDIM = 128
B_WIDTH = B_HEADS * B_HDIM
DILATED_PATTERNS = ((128, 1), (512, 4), (2048, 16))
ROPE_THETA = 10000.0
RMS_EPS = 1e-6
LN_EPS = 1e-5
NEG_BIG = -1e30
DEEPNORM_ALPHA = (2 * DEPTH) ** 0.25
LOG2E = 1.4426950408889634
A_SCALE = A_QK ** -0.5 * LOG2E
B_SCALE = B_HDIM ** -0.5 * LOG2E

LANES = 128
SEG = 1024
N_SEG = 6
LAT_PAD = SEG - (Q_LORA + KV_LORA + A_ROPE)
HALF_WIN = 64
DIL_SUBQ = 128
DIL_BQ = 512
VMEM_LIMIT = 56 * 1024 * 1024
MIX_DTYPE = BF16


def _cparams(semantics, flags=None):
    return pltpu.CompilerParams(dimension_semantics=semantics, vmem_limit_bytes=VMEM_LIMIT, flags=flags)


def _nt_dot(a, b):
    return lax.dot_general(a, b, (((1,), (1,)), ((), ())), preferred_element_type=F32)


SEG_MODES = ("lat", "rope", "rope", "plain", "silu", "silu")
SEG_SCALES = (1.0, B_SCALE, 1.0, 1.0, 1.0, 1.0)
N_DIL_OPERANDS = sum(m in ("rope", "plain") for m in SEG_MODES)


def _in_proj_kernel(x_ref, wl_ref, w_ref, cos_ref, sin_ref, *refs):
    outs, seg_ref = list(refs[:-1]), refs[-1]
    tm = x_ref.shape[0]
    xb = x_ref[...].astype(BF16)
    cos = cos_ref[...]
    sin = sin_ref[...]
    n_dil = 0
    for j, (mode, scale) in enumerate(zip(SEG_MODES, SEG_SCALES)):
        w = wl_ref[...] if j == 0 else w_ref[:, (j - 1) * SEG:j * SEG]
        acc = jnp.dot(xb, w, preferred_element_type=F32)
        if mode == "lat":
            outs.pop(0)[...] = acc
            continue
        if mode == "silu":
            outs.pop(0)[...] = (acc * jax.nn.sigmoid(acc)).astype(BF16)
            continue
        pattern_outs = [outs.pop(0) for _ in DILATED_PATTERNS]
        for h in range(B_HEADS):
            xh = acc[:, h * B_HDIM:(h + 1) * B_HDIM]
            if mode == "rope":
                xh = (xh * cos + pltpu.roll(xh, B_HDIM // 2, 1) * sin) * scale
            seg_ref[n_dil, h] = xh
            for o_ref, (_, dil) in zip(pattern_outs, DILATED_PATTERNS):
                if dil == 1:
                    o_ref[:, h * B_HDIM:(h + 1) * B_HDIM] = xh.astype(BF16)
                    continue
                for r in range(dil):
                    col = r * SEG + h * B_HDIM
                    o_ref[:, col:col + B_HDIM] = seg_ref[n_dil, h, pl.ds(r, tm // dil, stride=dil), :].astype(BF16)
        n_dil += 1


def _in_proj(x2, w_lat, w_rest, cos_b, sin_b, s, tm):
    n = x2.shape[0]
    n_pos = s // tm
    row = lambda i: (i, 0)
    out_specs, out_shape = [], []
    for mode in SEG_MODES:
        if mode in ("lat", "silu"):
            out_specs.append(pl.BlockSpec((tm, SEG), row))
            out_shape.append(jax.ShapeDtypeStruct((n, SEG), F32 if mode == "lat" else BF16))
        else:
            out_specs += [pl.BlockSpec((tm // d, d * SEG), row) for _, d in DILATED_PATTERNS]
            out_shape += [jax.ShapeDtypeStruct((n // d, d * SEG), BF16) for _, d in DILATED_PATTERNS]
    flat = pl.pallas_call(
        _in_proj_kernel,
        grid=(n // tm,),
        in_specs=[
            pl.BlockSpec((tm, D_MODEL), row),
            pl.BlockSpec((D_MODEL, SEG), lambda i: (0, 0), pipeline_mode=pl.Buffered(1)),
            pl.BlockSpec((D_MODEL, (N_SEG - 1) * SEG), lambda i: (0, 0), pipeline_mode=pl.Buffered(1)),
            pl.BlockSpec((tm, B_HDIM), lambda i: (i % n_pos, 0)),
            pl.BlockSpec((tm, B_HDIM), lambda i: (i % n_pos, 0)),
        ],
        out_specs=out_specs,
        out_shape=out_shape,
        scratch_shapes=[pltpu.VMEM((N_DIL_OPERANDS, B_HEADS, tm, B_HDIM), F32)],
        compiler_params=_cparams(("parallel",)),
        name="in_proj",
    )(x2, w_lat, w_rest, cos_b, sin_b)
    flat = list(flat)
    result = []
    for mode in SEG_MODES:
        if mode in ("lat", "silu"):
            result.append(flat.pop(0))
        else:
            result.append(tuple(flat.pop(0) for _ in DILATED_PATTERNS))
    return result


def _rms(x, g):
    return x * lax.rsqrt(jnp.mean(x * x, axis=-1, keepdims=True) + RMS_EPS) * g


def _mla_up_kernel(lat_ref, gq_ref, gkv_ref, wuqt_ref, wuk_ref, wuvt_ref,
                   cosk_ref, sink_ref, cost_ref, sint_ref, qt_ref, k_ref, vt_ref):
    lat = lat_ref[0]
    cqn = _rms(lat[:, :Q_LORA], gq_ref[...]).astype(BF16)
    ckvn = _rms(lat[:, Q_LORA:Q_LORA + KV_LORA], gkv_ref[...]).astype(BF16)
    kr = lat[:, Q_LORA + KV_LORA:Q_LORA + KV_LORA + LANES]
    lane = lax.broadcasted_iota(jnp.int32, kr.shape, 1)
    half = A_ROPE // 2
    rot = jnp.where(lane < half, pltpu.roll(kr, LANES - half, 1), pltpu.roll(kr, half, 1))
    kpe = (kr * cosk_ref[...] + rot * sink_ref[...]).astype(BF16)
    ct = cost_ref[...]
    st = sint_ref[...]
    for h in range(A_HEADS):
        qt = _nt_dot(wuqt_ref[h], cqn)
        x1 = qt[A_NOPE:A_NOPE + half]
        x2 = qt[A_NOPE + half:]
        qt_ref[0, h, 0:A_NOPE, :] = (qt[:A_NOPE] * A_SCALE).astype(BF16)
        qt_ref[0, h, A_NOPE:A_NOPE + half, :] = ((x1 * ct - x2 * st) * A_SCALE).astype(BF16)
        qt_ref[0, h, A_NOPE + half:, :] = ((x2 * ct + x1 * st) * A_SCALE).astype(BF16)
        k_ref[0, h, :, 0:A_NOPE] = jnp.dot(ckvn, wuk_ref[h], preferred_element_type=F32).astype(BF16)
        k_ref[0, h, :, A_NOPE:] = kpe[:, :A_ROPE]
        vt_ref[0, h, 0:A_VDIM, :] = _nt_dot(wuvt_ref[h], ckvn).astype(BF16)
        vt_ref[0, h, A_VDIM:, :] = jnp.ones((A_VROWS - A_VDIM, vt_ref.shape[3]), BF16)


def _mla_up(lat3, gq, gkv, wuqt, wuk, wuvt, cosk, sink, cost, sint, tm):
    b, s, _ = lat3.shape
    full = lambda *shape: pl.BlockSpec(shape, lambda bi, si: (0,) * len(shape))
    return pl.pallas_call(
        _mla_up_kernel,
        grid=(b, s // tm),
        in_specs=[
            pl.BlockSpec((1, tm, SEG), lambda bi, si: (bi, si, 0)),
            full(1, Q_LORA), full(1, KV_LORA),
            full(A_HEADS, A_QK, Q_LORA), full(A_HEADS, KV_LORA, A_NOPE), full(A_HEADS, A_VDIM, KV_LORA),
            pl.BlockSpec((tm, LANES), lambda bi, si: (si, 0)),
            pl.BlockSpec((tm, LANES), lambda bi, si: (si, 0)),
            pl.BlockSpec((A_ROPE // 2, tm), lambda bi, si: (0, si)),
            pl.BlockSpec((A_ROPE // 2, tm), lambda bi, si: (0, si)),
        ],
        out_specs=[
            pl.BlockSpec((1, A_HEADS, A_QK, tm), lambda bi, si: (bi, 0, 0, si)),
            pl.BlockSpec((1, A_HEADS, tm, A_QK), lambda bi, si: (bi, 0, si, 0)),
            pl.BlockSpec((1, A_HEADS, A_VROWS, tm), lambda bi, si: (bi, 0, 0, si)),
        ],
        out_shape=[
            jax.ShapeDtypeStruct((b, A_HEADS, A_QK, s), BF16),
            jax.ShapeDtypeStruct((b, A_HEADS, s, A_QK), BF16),
            jax.ShapeDtypeStruct((b, A_HEADS, A_VROWS, s), BF16),
        ],
        compiler_params=_cparams(("parallel", "parallel")),
        name="mla_up",
    )(lat3, gq, gkv, wuqt, wuk, wuvt, cosk, sink, cost, sint)


FLASH_BUFS = 2
FLASH_UNROLL = 4
FLASH_FLAGS = None
FLASH_HEADROOM = 64.0


def _mla_flash_kernel(qt_ref, k_ref, vt_ref, o_ref, s_ref, p_ref, acc_ref, *, bk, nk):
    bq = qt_ref.shape[3]

    def key_block(t):
        return k_ref[0, 0, pl.ds(pl.multiple_of(t * bk, bk), bk), :]

    def value_block(t):
        return vt_ref[0, 0, :, pl.ds(pl.multiple_of(t * bk, bk), bk)]

    def finish():
        acc = acc_ref[...]
        o_ref[0] = (acc[:A_VDIM] * (1.0 / acc[A_VDIM:A_VDIM + 1])).T.astype(o_ref.dtype)

    def scores_exp(t, slot, carry):
        m_cur, m_prev, _, excess = carry
        st = jnp.dot(key_block(t), qt_ref[0, 0], preferred_element_type=F32)
        cmax = jnp.max(st, axis=0, keepdims=True)
        p_ref[slot] = jnp.exp2((st - m_cur).astype(BF16))
        return jnp.maximum(m_cur, cmax), m_cur, jnp.exp2(m_prev - m_cur), jnp.maximum(excess, cmax - m_cur)

    def values(t, slot, alpha):
        acc_ref[...] = alpha * acc_ref[...] + jnp.dot(value_block(t), p_ref[slot], preferred_element_type=F32)

    def step(t, t_slot, carry):
        alpha_t = carry[2]
        carry = scores_exp(t + 1, (t_slot + 1) % FLASH_BUFS, carry)
        values(t, t_slot, alpha_t)
        return carry

    s0 = jnp.dot(key_block(0), qt_ref[0, 0], preferred_element_type=F32)
    s_ref[...] = s0
    m0 = jnp.max(s0, axis=0, keepdims=True)
    p_ref[0] = jnp.exp2((s_ref[...] - m0).astype(BF16))
    acc_ref[...] = jnp.zeros(acc_ref.shape, F32)
    carry = (m0, m0, jnp.ones((1, bq), F32), jnp.zeros((1, bq), F32))

    def group(j, carry):
        for c in range(FLASH_UNROLL):
            carry = step(FLASH_UNROLL * j + c, c % FLASH_BUFS, carry)
        return carry

    n_groups = (nk - 1) // FLASH_UNROLL
    carry = lax.fori_loop(0, n_groups, group, carry)
    for t in range(FLASH_UNROLL * n_groups, nk - 1):
        carry = step(t, t % FLASH_BUFS, carry)
    values(nk - 1, (nk - 1) % FLASH_BUFS, carry[2])
    finish()

    @pl.when(jnp.max(carry[3]) > FLASH_HEADROOM)
    def _():
        acc_ref[...] = jnp.zeros(acc_ref.shape, F32)

        def body(t, m):
            st = jnp.dot(key_block(t), qt_ref[0, 0], preferred_element_type=F32)
            m_new = jnp.maximum(m, jnp.max(st, axis=0, keepdims=True))
            p = jnp.exp2((st - m_new).astype(BF16))
            acc_ref[...] = jnp.exp2(m - m_new) * acc_ref[...] + jnp.dot(value_block(t), p, preferred_element_type=F32)
            return m_new

        lax.fori_loop(0, nk, body, jnp.full((1, bq), NEG_BIG, F32))
        finish()


def _mla_flash(qt, k, vt, bq, bk):
    b, h, _, s = qt.shape
    nk = s // bk
    assert nk >= 2
    kern = functools.partial(_mla_flash_kernel, bk=bk, nk=nk)
    return pl.pallas_call(
        kern,
        grid=(b, h, s // bq),
        in_specs=[
            pl.BlockSpec((1, 1, A_QK, bq), lambda bi, hi, qi: (bi, hi, 0, qi)),
            pl.BlockSpec((1, 1, s, A_QK), lambda bi, hi, qi: (bi, hi, 0, 0)),
            pl.BlockSpec((1, 1, A_VROWS, s), lambda bi, hi, qi: (bi, hi, 0, 0)),
        ],
        out_specs=pl.BlockSpec((1, bq, A_VDIM), lambda bi, hi, qi: (bi, qi, hi)),
        out_shape=jax.ShapeDtypeStruct((b, s, A_WIDTH), MIX_DTYPE),
        scratch_shapes=[pltpu.VMEM((bk, bq), F32), pltpu.VMEM((FLASH_BUFS, bk, bq), BF16),
                        pltpu.VMEM((A_VROWS, bq), F32)],
        compiler_params=_cparams(("parallel", "parallel", "arbitrary"), FLASH_FLAGS),
        name="mla_flash",
    )(qt, k, vt)


def _dilated_kernel(q_ref, kp_ref, kc_ref, kn_ref, vp_ref, vc_ref, vn_ref, o_ref, lse_ref, *, bq, nblk):
    i = pl.program_id(2)
    nkeys = DIL_SUBQ + 2 * HALF_WIN
    n_sub = bq // DIL_SUBQ
    a = lax.broadcasted_iota(jnp.int32, (DIL_SUBQ, nkeys), 0)
    c = lax.broadcasted_iota(jnp.int32, (DIL_SUBQ, nkeys), 1)
    band = (c >= a) & (c <= a + 2 * HALF_WIN)
    lane = lax.broadcasted_iota(jnp.int32, (DIL_SUBQ, LANES), 1)

    def window(p_ref, c_ref, n_ref, j, hs):
        lo = j * DIL_SUBQ - HALF_WIN
        hi = lo + nkeys
        parts = []
        if lo < 0:
            parts.append(p_ref[0, :, hs])
        parts.append(c_ref[0, max(lo, 0):min(hi, bq), hs])
        if hi > bq:
            parts.append(n_ref[0, :, hs])
        return jnp.concatenate(parts, axis=0) if len(parts) > 1 else parts[0]

    for j in range(n_sub):
        mask = band
        if j == 0:
            mask = mask & ((c >= HALF_WIN) | (i > 0))
        if j == n_sub - 1:
            mask = mask & ((c < DIL_SUBQ + HALF_WIN) | (i < nblk - 1))
        rows = slice(j * DIL_SUBQ, (j + 1) * DIL_SUBQ)
        lse_all = jnp.zeros((DIL_SUBQ, LANES), F32)
        for h in range(B_HEADS):
            hs = slice(h * B_HDIM, (h + 1) * B_HDIM)
            k = window(kp_ref, kc_ref, kn_ref, j, hs)
            v = window(vp_ref, vc_ref, vn_ref, j, hs)
            sc = jnp.where(mask, _nt_dot(q_ref[0, rows, hs], k), NEG_BIG)
            m = jnp.max(sc, axis=1, keepdims=True)
            p = jnp.exp2(sc - m).astype(BF16)
            both = jnp.dot(p, jnp.concatenate([v, jnp.ones_like(v)], axis=1), preferred_element_type=F32)
            den = both[:, B_HDIM:]
            o_ref[0, rows, hs] = (both[:, :B_HDIM] * (1.0 / den)).astype(o_ref.dtype)
            lse_all = jnp.where(lane == h, m + jnp.log2(den), lse_all)
        lse_ref[0, rows, :] = lse_all


def _dilated(qv, kv, vv, b, dil, bq):
    m_len = qv.shape[0] // b
    nblk = m_len // bq
    sub = bq // HALF_WIN
    n_half = m_len // HALF_WIN
    view = lambda t: t.reshape(b, m_len, dil * B_WIDTH)
    cur = pl.BlockSpec((1, bq, B_WIDTH), lambda bi, r, i: (bi, i, r))
    prev = pl.BlockSpec((1, HALF_WIN, B_WIDTH), lambda bi, r, i: (bi, jnp.maximum(i * sub - 1, 0), r))
    nxt = pl.BlockSpec((1, HALF_WIN, B_WIDTH), lambda bi, r, i: (bi, jnp.minimum((i + 1) * sub, n_half - 1), r))
    kern = functools.partial(_dilated_kernel, bq=bq, nblk=nblk)
    o, lse = pl.pallas_call(
        kern,
        grid=(b, dil, nblk),
        in_specs=[cur, prev, cur, nxt, prev, cur, nxt],
        out_specs=[cur, pl.BlockSpec((1, bq, LANES), lambda bi, r, i: (bi, i, r))],
        out_shape=[jax.ShapeDtypeStruct((b, m_len, dil * B_WIDTH), MIX_DTYPE),
                   jax.ShapeDtypeStruct((b, m_len, dil * LANES), F32)],
        compiler_params=_cparams(("parallel", "parallel", "parallel")),
        name=f"dilated_d{dil}",
    )(view(qv), view(kv), view(kv), view(kv), view(vv), view(vv), view(vv))
    return o.reshape(b * m_len, dil * B_WIDTH), lse.reshape(b * m_len, dil * LANES)


OUT_GROUPS = 2


def _out_proj_kernel(oa_ref, o1_ref, o2_ref, o3_ref, l1_ref, l2_ref, l3_ref, ga_ref, gb_ref, x_ref,
                     w_ref, goa_ref, gob_ref, lng_ref, lnb_ref, y_ref, on_ref, ln_ref):
    tm = oa_ref.shape[0]
    for i, (o_ref, l_ref, (_, dil)) in enumerate(zip((o1_ref, o2_ref, o3_ref), (l1_ref, l2_ref, l3_ref),
                                                     DILATED_PATTERNS)):
        for r in range(dil):
            rows = pl.ds(r, tm // dil, stride=dil) if dil > 1 else slice(None)
            ln_ref[i, rows, :] = l_ref[:, r * LANES:(r + 1) * LANES]
            for h in range(B_HEADS):
                col = r * B_WIDTH + h * B_HDIM
                on_ref[i, h, rows, :] = o_ref[:, col:col + B_HDIM].astype(F32)
    for g in range(OUT_GROUPS):
        rows = slice(g * tm // OUT_GROUPS, (g + 1) * tm // OUT_GROUPS)
        za = (_rms(oa_ref[rows, :].astype(F32), goa_ref[...]) * ga_ref[rows, :].astype(F32)).astype(BF16)
        l1, l2, l3 = ln_ref[0, rows, :], ln_ref[1, rows, :], ln_ref[2, rows, :]
        mx = jnp.maximum(jnp.maximum(l1, l2), l3)
        e1, e2, e3 = jnp.exp2(l1 - mx), jnp.exp2(l2 - mx), jnp.exp2(l3 - mx)
        inv = 1.0 / (e1 + e2 + e3)
        w1, w2, w3 = e1 * inv, e2 * inv, e3 * inv
        parts = []
        for h in range(B_HEADS):
            parts.append(w1[:, h:h + 1] * on_ref[0, h, rows, :] + w2[:, h:h + 1] * on_ref[1, h, rows, :]
                         + w3[:, h:h + 1] * on_ref[2, h, rows, :])
        ob = jnp.concatenate(parts, axis=1)
        zb = (_rms(ob, gob_ref[...]) * gb_ref[rows, :].astype(F32)).astype(BF16)
        out = (jnp.dot(za, w_ref[:A_WIDTH, :], preferred_element_type=F32)
               + jnp.dot(zb, w_ref[A_WIDTH:, :], preferred_element_type=F32))
        r = DEEPNORM_ALPHA * x_ref[rows, :] + out
        mu = jnp.mean(r, axis=-1, keepdims=True)
        d = r - mu
        var = jnp.mean(d * d, axis=-1, keepdims=True)
        y_ref[rows, :] = d * lax.rsqrt(var + LN_EPS) * lng_ref[...] + lnb_ref[...]


def _out_proj(oa, obs, lses, ga, gb, x2, w_out, goa, gob, lng, lnb, tm):
    n = x2.shape[0]
    row = lambda w: pl.BlockSpec((tm, w), lambda i: (i, 0))
    view = lambda w: [pl.BlockSpec((tm // d, d * w), lambda i: (i, 0)) for _, d in DILATED_PATTERNS]
    full = lambda *shape: pl.BlockSpec(shape, lambda i: (0,) * len(shape))
    n_pat = len(DILATED_PATTERNS)
    return pl.pallas_call(
        _out_proj_kernel,
        grid=(n // tm,),
        in_specs=[row(A_WIDTH)] + view(B_WIDTH) + view(LANES) + [row(A_WIDTH), row(B_WIDTH), row(D_MODEL),
                  full(A_WIDTH + B_WIDTH, D_MODEL), full(1, A_WIDTH), full(1, B_WIDTH),
                  full(1, D_MODEL), full(1, D_MODEL)],
        out_specs=row(D_MODEL),
        out_shape=jax.ShapeDtypeStruct((n, D_MODEL), F32),
        scratch_shapes=[pltpu.VMEM((n_pat, B_HEADS, tm, B_HDIM), F32), pltpu.VMEM((n_pat, tm, LANES), F32)],
        compiler_params=_cparams(("parallel",)),
        name="out_proj",
    )(oa, *obs, *lses, ga, gb, x2, w_out, goa, gob, lng, lnb)


def _rope_tables(s):
    pos = jnp.arange(s, dtype=F32)

    def cos_sin(d):
        half = d // 2
        inv = ROPE_THETA ** (-jnp.arange(half, dtype=F32) * 2.0 / d)
        ang = pos[:, None] * inv[None, :]
        return jnp.cos(ang), jnp.sin(ang)

    cb, sb = cos_sin(B_HDIM)
    cos_b = jnp.concatenate([cb, cb], axis=1)
    sin_b = jnp.concatenate([-sb, sb], axis=1)
    ca, sa = cos_sin(A_ROPE)
    zeros = jnp.zeros((s, LANES - A_ROPE), F32)
    cos_k = jnp.concatenate([ca, ca, zeros], axis=1)
    sin_k = jnp.concatenate([-sa, sa, zeros], axis=1)
    return cos_b, sin_b, cos_k, sin_k, ca.T, sa.T


def _prep_weights(w_in, g_qn, w_uq, g_kvn, w_ukv, g_oa, g_ob, w_out, ln_g, ln_b):
    n_lat = Q_LORA + KV_LORA + A_ROPE
    w_lat = jnp.pad(w_in[:, :n_lat], ((0, 0), (0, LAT_PAD))).astype(BF16)
    w_rest = w_in[:, n_lat:].astype(BF16)
    wuqt = w_uq.reshape(Q_LORA, A_HEADS, A_QK).transpose(1, 2, 0).astype(BF16)
    wukv = w_ukv.reshape(KV_LORA, A_HEADS, A_NOPE + A_VDIM)
    wuk = wukv[:, :, :A_NOPE].transpose(1, 0, 2).astype(BF16)
    wuvt = wukv[:, :, A_NOPE:].transpose(1, 2, 0).astype(BF16)
    return dict(w_lat=w_lat, w_rest=w_rest, gq=g_qn[None, :], gkv=g_kvn[None, :], wuqt=wuqt, wuk=wuk, wuvt=wuvt,
                goa=g_oa[None, :], gob=g_ob[None, :], w_out=w_out.astype(BF16),
                lng=ln_g[None, :], lnb=ln_b[None, :])


def _tiles(s):
    pick = lambda pref, n: pref if n % pref == 0 else n
    return dict(tm_in=pick(256, s), tm_up=pick(512, s), bq=pick(2048, s), bk=pick(512, s), tm_out=pick(256, s))


def _encoder_layer(x, p):
    b, s, _ = x.shape
    longest = max(w // 2 for w, _ in DILATED_PATTERNS)
    assert s % longest == 0, "sequence length must be a multiple of the largest dilated chunk"
    t = _tiles(s)
    cos_b, sin_b, cos_k, sin_k, cos_t, sin_t = _rope_tables(s)
    x2 = x.reshape(b * s, D_MODEL)
    lat, qvs, kvs, vvs, ga, gb = _in_proj(x2, p["w_lat"], p["w_rest"], cos_b, sin_b, s, t["tm_in"])
    qt, k, vt = _mla_up(lat.reshape(b, s, SEG), p["gq"], p["gkv"], p["wuqt"], p["wuk"], p["wuvt"],
                        cos_k, sin_k, cos_t, sin_t, t["tm_up"])
    oa = _mla_flash(qt, k, vt, t["bq"], t["bk"]).reshape(b * s, A_WIDTH)
    obs, lses = [], []
    for i, (window, dil) in enumerate(DILATED_PATTERNS):
        assert window // (2 * dil) == HALF_WIN
        o, lse = _dilated(qvs[i], kvs[i], vvs[i], b, dil, min(DIL_BQ, s // dil))
        obs.append(o)
        lses.append(lse)
    y = _out_proj(oa, obs, lses, ga, gb, x2, p["w_out"], p["goa"], p["gob"], p["lng"], p["lnb"], t["tm_out"])
    return y.reshape(b, s, D_MODEL)


def kernel(x_prompt, x_sample, w_in, g_qn, w_uq, g_kvn, w_ukv, g_oa, g_ob, w_out, ln_g, ln_b):
    layers = [_prep_weights(w_in[l], g_qn[l], w_uq[l], g_kvn[l], w_ukv[l], g_oa[l], g_ob[l],
                            w_out[l], ln_g[l], ln_b[l]) for l in range(DEPTH)]

    def trunk(x):
        for p in layers:
            x = _encoder_layer(x, p)
        return x

    return (trunk(x_prompt), trunk(x_sample))
```

```python
import functools

import jax
import jax.numpy as jnp
from jax import lax
from jax.experimental import pallas as pl
from jax.experimental.pallas import tpu as pltpu

F32 = jnp.float32
BF16 = jnp.bfloat16

D_MODEL = 2048
DEPTH = 1
A_HEADS = 8
A_NOPE = 128
A_ROPE = 64
A_VDIM = 128
A_QK = A_NOPE + A_ROPE
A_WIDTH = A_HEADS * A_VDIM
A_VROWS = A_VDIM + 16
Q_LORA = 512
KV_LORA = 256
B_HEADS = 8
B_HDIM = 128
B_WIDTH = B_HEADS * B_HDIM
DILATED_PATTERNS = ((128, 1), (512, 4), (2048, 16))
ROPE_THETA = 10000.0
RMS_EPS = 1e-6
LN_EPS = 1e-5
NEG_BIG = -1e30
DEEPNORM_ALPHA = (2 * DEPTH) ** 0.25
LOG2E = 1.4426950408889634
A_SCALE = A_QK ** -0.5 * LOG2E
B_SCALE = B_HDIM ** -0.5 * LOG2E

LANES = 128
SEG = 1024
N_SEG = 6
LAT_PAD = SEG - (Q_LORA + KV_LORA + A_ROPE)
HALF_WIN = 64
DIL_SUBQ = 128
DIL_BQ = 512
VMEM_LIMIT = 56 * 1024 * 1024
MIX_DTYPE = BF16


def _cparams(semantics, flags=None):
    return pltpu.CompilerParams(dimension_semantics=semantics, vmem_limit_bytes=VMEM_LIMIT, flags=flags)


def _nt_dot(a, b):
    return lax.dot_general(a, b, (((1,), (1,)), ((), ())), preferred_element_type=F32)


SEG_MODES = ("lat", "rope", "rope", "plain", "silu", "silu")
SEG_SCALES = (1.0, B_SCALE, 1.0, 1.0, 1.0, 1.0)
N_DIL_OPERANDS = sum(m in ("rope", "plain") for m in SEG_MODES)


def _in_proj_kernel(x_ref, wl_ref, w_ref, cos_ref, sin_ref, *refs):
    outs, seg_ref = list(refs[:-1]), refs[-1]
    tm = x_ref.shape[0]
    xb = x_ref[...].astype(BF16)
    cos = cos_ref[...]
    sin = sin_ref[...]
    n_dil = 0
    for j, (mode, scale) in enumerate(zip(SEG_MODES, SEG_SCALES)):
        w = wl_ref[...] if j == 0 else w_ref[:, (j - 1) * SEG:j * SEG]
        acc = jnp.dot(xb, w, preferred_element_type=F32)
        if mode == "lat":
            outs.pop(0)[...] = acc
            continue
        if mode == "silu":
            outs.pop(0)[...] = (acc * jax.nn.sigmoid(acc)).astype(BF16)
            continue
        pattern_outs = [outs.pop(0) for _ in DILATED_PATTERNS]
        for h in range(B_HEADS):
            xh = acc[:, h * B_HDIM:(h + 1) * B_HDIM]
            if mode == "rope":
                xh = (xh * cos + pltpu.roll(xh, B_HDIM // 2, 1) * sin) * scale
            seg_ref[n_dil, h] = xh
            for o_ref, (_, dil) in zip(pattern_outs, DILATED_PATTERNS):
                if dil == 1:
                    o_ref[:, h * B_HDIM:(h + 1) * B_HDIM] = xh.astype(BF16)
                    continue
                for r in range(dil):
                    col = r * SEG + h * B_HDIM
                    o_ref[:, col:col + B_HDIM] = seg_ref[n_dil, h, pl.ds(r, tm // dil, stride=dil), :].astype(BF16)
        n_dil += 1


def _in_proj(x2, w_lat, w_rest, cos_b, sin_b, s, tm):
    n = x2.shape[0]
    n_pos = s // tm
    row = lambda i: (i, 0)
    out_specs, out_shape = [], []
    for mode in SEG_MODES:
        if mode in ("lat", "silu"):
            out_specs.append(pl.BlockSpec((tm, SEG), row))
            out_shape.append(jax.ShapeDtypeStruct((n, SEG), F32 if mode == "lat" else BF16))
        else:
            out_specs += [pl.BlockSpec((tm // d, d * SEG), row) for _, d in DILATED_PATTERNS]
            out_shape += [jax.ShapeDtypeStruct((n // d, d * SEG), BF16) for _, d in DILATED_PATTERNS]
    flat = pl.pallas_call(
        _in_proj_kernel,
        grid=(n // tm,),
        in_specs=[
            pl.BlockSpec((tm, D_MODEL), row),
            pl.BlockSpec((D_MODEL, SEG), lambda i: (0, 0), pipeline_mode=pl.Buffered(1)),
            pl.BlockSpec((D_MODEL, (N_SEG - 1) * SEG), lambda i: (0, 0), pipeline_mode=pl.Buffered(1)),
            pl.BlockSpec((tm, B_HDIM), lambda i: (i % n_pos, 0)),
            pl.BlockSpec((tm, B_HDIM), lambda i: (i % n_pos, 0)),
        ],
        out_specs=out_specs,
        out_shape=out_shape,
        scratch_shapes=[pltpu.VMEM((N_DIL_OPERANDS, B_HEADS, tm, B_HDIM), F32)],
        compiler_params=_cparams(("parallel",)),
        name="in_proj",
    )(x2, w_lat, w_rest, cos_b, sin_b)
    flat = list(flat)
    result = []
    for mode in SEG_MODES:
        if mode in ("lat", "silu"):
            result.append(flat.pop(0))
        else:
            result.append(tuple(flat.pop(0) for _ in DILATED_PATTERNS))
    return result


def _rms(x, g):
    return x * lax.rsqrt(jnp.mean(x * x, axis=-1, keepdims=True) + RMS_EPS) * g


def _mla_up_kernel(lat_ref, gq_ref, gkv_ref, wuqt_ref, wuk_ref, wuvt_ref,
                   cosk_ref, sink_ref, cost_ref, sint_ref, qt_ref, k_ref, vt_ref):
    lat = lat_ref[0]
    cqn = _rms(lat[:, :Q_LORA], gq_ref[...]).astype(BF16)
    ckvn = _rms(lat[:, Q_LORA:Q_LORA + KV_LORA], gkv_ref[...]).astype(BF16)
    kr = lat[:, Q_LORA + KV_LORA:Q_LORA + KV_LORA + LANES]
    lane = lax.broadcasted_iota(jnp.int32, kr.shape, 1)
    half = A_ROPE // 2
    rot = jnp.where(lane < half, pltpu.roll(kr, LANES - half, 1), pltpu.roll(kr, half, 1))
    kpe = (kr * cosk_ref[...] + rot * sink_ref[...]).astype(BF16)
    ct = cost_ref[...]
    st = sint_ref[...]
    for h in range(A_HEADS):
        qt = _nt_dot(wuqt_ref[h], cqn)
        x1 = qt[A_NOPE:A_NOPE + half]
        x2 = qt[A_NOPE + half:]
        qt_ref[0, h, 0:A_NOPE, :] = (qt[:A_NOPE] * A_SCALE).astype(BF16)
        qt_ref[0, h, A_NOPE:A_NOPE + half, :] = ((x1 * ct - x2 * st) * A_SCALE).astype(BF16)
        qt_ref[0, h, A_NOPE + half:, :] = ((x2 * ct + x1 * st) * A_SCALE).astype(BF16)
        k_ref[0, h, :, 0:A_NOPE] = jnp.dot(ckvn, wuk_ref[h], preferred_element_type=F32).astype(BF16)
        k_ref[0, h, :, A_NOPE:] = kpe[:, :A_ROPE]
        vt_ref[0, h, 0:A_VDIM, :] = _nt_dot(wuvt_ref[h], ckvn).astype(BF16)
        vt_ref[0, h, A_VDIM:, :] = jnp.ones((A_VROWS - A_VDIM, vt_ref.shape[3]), BF16)


def _mla_up(lat3, gq, gkv, wuqt, wuk, wuvt, cosk, sink, cost, sint, tm):
    b, s, _ = lat3.shape
    full = lambda *shape: pl.BlockSpec(shape, lambda bi, si: (0,) * len(shape))
    return pl.pallas_call(
        _mla_up_kernel,
        grid=(b, s // tm),
        in_specs=[
            pl.BlockSpec((1, tm, SEG), lambda bi, si: (bi, si, 0)),
            full(1, Q_LORA), full(1, KV_LORA),
            full(A_HEADS, A_QK, Q_LORA), full(A_HEADS, KV_LORA, A_NOPE), full(A_HEADS, A_VDIM, KV_LORA),
            pl.BlockSpec((tm, LANES), lambda bi, si: (si, 0)),
            pl.BlockSpec((tm, LANES), lambda bi, si: (si, 0)),
            pl.BlockSpec((A_ROPE // 2, tm), lambda bi, si: (0, si)),
            pl.BlockSpec((A_ROPE // 2, tm), lambda bi, si: (0, si)),
        ],
        out_specs=[
            pl.BlockSpec((1, A_HEADS, A_QK, tm), lambda bi, si: (bi, 0, 0, si)),
            pl.BlockSpec((1, A_HEADS, tm, A_QK), lambda bi, si: (bi, 0, si, 0)),
            pl.BlockSpec((1, A_HEADS, A_VROWS, tm), lambda bi, si: (bi, 0, 0, si)),
        ],
        out_shape=[
            jax.ShapeDtypeStruct((b, A_HEADS, A_QK, s), BF16),
            jax.ShapeDtypeStruct((b, A_HEADS, s, A_QK), BF16),
            jax.ShapeDtypeStruct((b, A_HEADS, A_VROWS, s), BF16),
        ],
        compiler_params=_cparams(("parallel", "parallel")),
        name="mla_up",
    )(lat3, gq, gkv, wuqt, wuk, wuvt, cosk, sink, cost, sint)


FLASH_BUFS = 2
FLASH_UNROLL = 4
FLASH_FLAGS = None
FLASH_BQ_MAX = 4096
FLASH_HEADROOM = 64.0


def _mla_flash_kernel(qt_ref, k_ref, vt_ref, o_ref, s_ref, p_ref, acc_ref, *, bk, nk):
    bq = qt_ref.shape[3]

    def key_block(t):
        return k_ref[0, 0, pl.ds(pl.multiple_of(t * bk, bk), bk), :]

    def value_block(t):
        return vt_ref[0, 0, :, pl.ds(pl.multiple_of(t * bk, bk), bk)]

    def finish():
        acc = acc_ref[...]
        o_ref[0] = (acc[:A_VDIM] * (1.0 / acc[A_VDIM:A_VDIM + 1])).T.astype(o_ref.dtype)

    def scores_exp(t, slot, carry):
        m_cur, m_prev, _, excess = carry
        st = jnp.dot(key_block(t), qt_ref[0, 0], preferred_element_type=F32)
        cmax = jnp.max(st, axis=0, keepdims=True)
        p_ref[slot] = jnp.exp2((st - m_cur).astype(BF16))
        return jnp.maximum(m_cur, cmax), m_cur, jnp.exp2(m_prev - m_cur), jnp.maximum(excess, cmax - m_cur)

    def values(t, slot, alpha):
        acc_ref[...] = alpha * acc_ref[...] + jnp.dot(value_block(t), p_ref[slot], preferred_element_type=F32)

    def step(t, t_slot, carry):
        alpha_t = carry[2]
        carry = scores_exp(t + 1, (t_slot + 1) % FLASH_BUFS, carry)
        values(t, t_slot, alpha_t)
        return carry

    s0 = jnp.dot(key_block(0), qt_ref[0, 0], preferred_element_type=F32)
    s_ref[...] = s0
    m0 = jnp.max(s0, axis=0, keepdims=True)
    p_ref[0] = jnp.exp2((s_ref[...] - m0).astype(BF16))
    acc_ref[...] = jnp.zeros(acc_ref.shape, F32)
    carry = (m0, m0, jnp.ones((1, bq), F32), jnp.zeros((1, bq), F32))

    def group(j, carry):
        for c in range(FLASH_UNROLL):
            carry = step(FLASH_UNROLL * j + c, c % FLASH_BUFS, carry)
        return carry

    n_groups = (nk - 1) // FLASH_UNROLL
    carry = lax.fori_loop(0, n_groups, group, carry)
    for t in range(FLASH_UNROLL * n_groups, nk - 1):
        carry = step(t, t % FLASH_BUFS, carry)
    values(nk - 1, (nk - 1) % FLASH_BUFS, carry[2])
    finish()

    @pl.when(jnp.max(carry[3]) > FLASH_HEADROOM)
    def _():
        acc_ref[...] = jnp.zeros(acc_ref.shape, F32)

        def body(t, m):
            st = jnp.dot(key_block(t), qt_ref[0, 0], preferred_element_type=F32)
            m_new = jnp.maximum(m, jnp.max(st, axis=0, keepdims=True))
            p = jnp.exp2((st - m_new).astype(BF16))
            acc_ref[...] = jnp.exp2(m - m_new) * acc_ref[...] + jnp.dot(value_block(t), p, preferred_element_type=F32)
            return m_new

        lax.fori_loop(0, nk, body, jnp.full((1, bq), NEG_BIG, F32))
        finish()


def _mla_flash(qt, k, vt, bq, bk):
    b, h, _, s = qt.shape
    nk = s // bk
    assert nk >= 2
    kern = functools.partial(_mla_flash_kernel, bk=bk, nk=nk)
    return pl.pallas_call(
        kern,
        grid=(b, h, s // bq),
        in_specs=[
            pl.BlockSpec((1, 1, A_QK, bq), lambda bi, hi, qi: (bi, hi, 0, qi)),
            pl.BlockSpec((1, 1, s, A_QK), lambda bi, hi, qi: (bi, hi, 0, 0)),
            pl.BlockSpec((1, 1, A_VROWS, s), lambda bi, hi, qi: (bi, hi, 0, 0)),
        ],
        out_specs=pl.BlockSpec((1, bq, A_VDIM), lambda bi, hi, qi: (bi, qi, hi)),
        out_shape=jax.ShapeDtypeStruct((b, s, A_WIDTH), MIX_DTYPE),
        scratch_shapes=[pltpu.VMEM((bk, bq), F32), pltpu.VMEM((FLASH_BUFS, bk, bq), BF16),
                        pltpu.VMEM((A_VROWS, bq), F32)],
        compiler_params=_cparams(("parallel", "parallel", "arbitrary"), FLASH_FLAGS),
        name="mla_flash",
    )(qt, k, vt)


def _dilated_kernel(q_ref, kp_ref, kc_ref, kn_ref, vp_ref, vc_ref, vn_ref, o_ref, lse_ref, *, bq, nblk):
    i = pl.program_id(2)
    nkeys = DIL_SUBQ + 2 * HALF_WIN
    n_sub = bq // DIL_SUBQ
    a = lax.broadcasted_iota(jnp.int32, (DIL_SUBQ, nkeys), 0)
    c = lax.broadcasted_iota(jnp.int32, (DIL_SUBQ, nkeys), 1)
    band = (c >= a) & (c <= a + 2 * HALF_WIN)
    lane = lax.broadcasted_iota(jnp.int32, (DIL_SUBQ, LANES), 1)

    def window(p_ref, c_ref, n_ref, j, hs):
        lo = j * DIL_SUBQ - HALF_WIN
        hi = lo + nkeys
        parts = []
        if lo < 0:
            parts.append(p_ref[0, :, hs])
        parts.append(c_ref[0, max(lo, 0):min(hi, bq), hs])
        if hi > bq:
            parts.append(n_ref[0, :, hs])
        return jnp.concatenate(parts, axis=0) if len(parts) > 1 else parts[0]

    for j in range(n_sub):
        mask = band
        if j == 0:
            mask = mask & ((c >= HALF_WIN) | (i > 0))
        if j == n_sub - 1:
            mask = mask & ((c < DIL_SUBQ + HALF_WIN) | (i < nblk - 1))
        rows = slice(j * DIL_SUBQ, (j + 1) * DIL_SUBQ)
        lse_all = jnp.zeros((DIL_SUBQ, LANES), F32)
        for h in range(B_HEADS):
            hs = slice(h * B_HDIM, (h + 1) * B_HDIM)
            k = window(kp_ref, kc_ref, kn_ref, j, hs)
            v = window(vp_ref, vc_ref, vn_ref, j, hs)
            sc = jnp.where(mask, _nt_dot(q_ref[0, rows, hs], k), NEG_BIG)
            m = jnp.max(sc, axis=1, keepdims=True)
            p = jnp.exp2(sc - m).astype(BF16)
            both = jnp.dot(p, jnp.concatenate([v, jnp.ones_like(v)], axis=1), preferred_element_type=F32)
            den = both[:, B_HDIM:]
            o_ref[0, rows, hs] = (both[:, :B_HDIM] * (1.0 / den)).astype(o_ref.dtype)
            lse_all = jnp.where(lane == h, m + jnp.log2(den), lse_all)
        lse_ref[0, rows, :] = lse_all


def _dilated(qv, kv, vv, b, dil, bq):
    m_len = qv.shape[0] // b
    nblk = m_len // bq
    sub = bq // HALF_WIN
    n_half = m_len // HALF_WIN
    view = lambda t: t.reshape(b, m_len, dil * B_WIDTH)
    cur = pl.BlockSpec((1, bq, B_WIDTH), lambda bi, r, i: (bi, i, r))
    prev = pl.BlockSpec((1, HALF_WIN, B_WIDTH), lambda bi, r, i: (bi, jnp.maximum(i * sub - 1, 0), r))
    nxt = pl.BlockSpec((1, HALF_WIN, B_WIDTH), lambda bi, r, i: (bi, jnp.minimum((i + 1) * sub, n_half - 1), r))
    kern = functools.partial(_dilated_kernel, bq=bq, nblk=nblk)
    o, lse = pl.pallas_call(
        kern,
        grid=(b, dil, nblk),
        in_specs=[cur, prev, cur, nxt, prev, cur, nxt],
        out_specs=[cur, pl.BlockSpec((1, bq, LANES), lambda bi, r, i: (bi, i, r))],
        out_shape=[jax.ShapeDtypeStruct((b, m_len, dil * B_WIDTH), MIX_DTYPE),
                   jax.ShapeDtypeStruct((b, m_len, dil * LANES), F32)],
        compiler_params=_cparams(("parallel", "parallel", "parallel")),
        name=f"dilated_d{dil}",
    )(view(qv), view(kv), view(kv), view(kv), view(vv), view(vv), view(vv))
    return o.reshape(b * m_len, dil * B_WIDTH), lse.reshape(b * m_len, dil * LANES)


OUT_GROUPS = 2


def _out_proj_kernel(oa_ref, o1_ref, o2_ref, o3_ref, l1_ref, l2_ref, l3_ref, ga_ref, gb_ref, x_ref,
                     w_ref, goa_ref, gob_ref, lng_ref, lnb_ref, y_ref, on_ref, ln_ref):
    tm = oa_ref.shape[0]
    for i, (o_ref, l_ref, (_, dil)) in enumerate(zip((o1_ref, o2_ref, o3_ref), (l1_ref, l2_ref, l3_ref),
                                                     DILATED_PATTERNS)):
        for r in range(dil):
            rows = pl.ds(r, tm // dil, stride=dil) if dil > 1 else slice(None)
            ln_ref[i, rows, :] = l_ref[:, r * LANES:(r + 1) * LANES]
            for h in range(B_HEADS):
                col = r * B_WIDTH + h * B_HDIM
                on_ref[i, h, rows, :] = o_ref[:, col:col + B_HDIM].astype(F32)
    for g in range(OUT_GROUPS):
        rows = slice(g * tm // OUT_GROUPS, (g + 1) * tm // OUT_GROUPS)
        za = (_rms(oa_ref[rows, :].astype(F32), goa_ref[...]) * ga_ref[rows, :].astype(F32)).astype(BF16)
        l1, l2, l3 = ln_ref[0, rows, :], ln_ref[1, rows, :], ln_ref[2, rows, :]
        mx = jnp.maximum(jnp.maximum(l1, l2), l3)
        e1, e2, e3 = jnp.exp2(l1 - mx), jnp.exp2(l2 - mx), jnp.exp2(l3 - mx)
        inv = 1.0 / (e1 + e2 + e3)
        w1, w2, w3 = e1 * inv, e2 * inv, e3 * inv
        parts = []
        for h in range(B_HEADS):
            parts.append(w1[:, h:h + 1] * on_ref[0, h, rows, :] + w2[:, h:h + 1] * on_ref[1, h, rows, :]
                         + w3[:, h:h + 1] * on_ref[2, h, rows, :])
        ob = jnp.concatenate(parts, axis=1)
        zb = (_rms(ob, gob_ref[...]) * gb_ref[rows, :].astype(F32)).astype(BF16)
        out = (jnp.dot(za, w_ref[:A_WIDTH, :], preferred_element_type=F32)
               + jnp.dot(zb, w_ref[A_WIDTH:, :], preferred_element_type=F32))
        r = DEEPNORM_ALPHA * x_ref[rows, :] + out
        mu = jnp.mean(r, axis=-1, keepdims=True)
        d = r - mu
        var = jnp.mean(d * d, axis=-1, keepdims=True)
        y_ref[rows, :] = d * lax.rsqrt(var + LN_EPS) * lng_ref[...] + lnb_ref[...]


def _out_proj(oa, obs, lses, ga, gb, x2, w_out, goa, gob, lng, lnb, tm):
    n = x2.shape[0]
    row = lambda w: pl.BlockSpec((tm, w), lambda i: (i, 0))
    view = lambda w: [pl.BlockSpec((tm // d, d * w), lambda i: (i, 0)) for _, d in DILATED_PATTERNS]
    full = lambda *shape: pl.BlockSpec(shape, lambda i: (0,) * len(shape))
    n_pat = len(DILATED_PATTERNS)
    return pl.pallas_call(
        _out_proj_kernel,
        grid=(n // tm,),
        in_specs=[row(A_WIDTH)] + view(B_WIDTH) + view(LANES) + [row(A_WIDTH), row(B_WIDTH), row(D_MODEL),
                  full(A_WIDTH + B_WIDTH, D_MODEL), full(1, A_WIDTH), full(1, B_WIDTH),
                  full(1, D_MODEL), full(1, D_MODEL)],
        out_specs=row(D_MODEL),
        out_shape=jax.ShapeDtypeStruct((n, D_MODEL), F32),
        scratch_shapes=[pltpu.VMEM((n_pat, B_HEADS, tm, B_HDIM), F32), pltpu.VMEM((n_pat, tm, LANES), F32)],
        compiler_params=_cparams(("parallel",)),
        name="out_proj",
    )(oa, *obs, *lses, ga, gb, x2, w_out, goa, gob, lng, lnb)


def _rope_tables(s):
    pos = jnp.arange(s, dtype=F32)

    def cos_sin(d):
        half = d // 2
        inv = ROPE_THETA ** (-jnp.arange(half, dtype=F32) * 2.0 / d)
        ang = pos[:, None] * inv[None, :]
        return jnp.cos(ang), jnp.sin(ang)

    cb, sb = cos_sin(B_HDIM)
    cos_b = jnp.concatenate([cb, cb], axis=1)
    sin_b = jnp.concatenate([-sb, sb], axis=1)
    ca, sa = cos_sin(A_ROPE)
    zeros = jnp.zeros((s, LANES - A_ROPE), F32)
    cos_k = jnp.concatenate([ca, ca, zeros], axis=1)
    sin_k = jnp.concatenate([-sa, sa, zeros], axis=1)
    return cos_b, sin_b, cos_k, sin_k, ca.T, sa.T


def _prep_weights(w_in, g_qn, w_uq, g_kvn, w_ukv, g_oa, g_ob, w_out, ln_g, ln_b):
    n_lat = Q_LORA + KV_LORA + A_ROPE
    w_lat = jnp.pad(w_in[:, :n_lat], ((0, 0), (0, LAT_PAD))).astype(BF16)
    w_rest = w_in[:, n_lat:].astype(BF16)
    wuqt = w_uq.reshape(Q_LORA, A_HEADS, A_QK).transpose(1, 2, 0).astype(BF16)
    wukv = w_ukv.reshape(KV_LORA, A_HEADS, A_NOPE + A_VDIM)
    wuk = wukv[:, :, :A_NOPE].transpose(1, 0, 2).astype(BF16)
    wuvt = wukv[:, :, A_NOPE:].transpose(1, 2, 0).astype(BF16)
    return dict(w_lat=w_lat, w_rest=w_rest, gq=g_qn[None, :], gkv=g_kvn[None, :], wuqt=wuqt, wuk=wuk, wuvt=wuvt,
                goa=g_oa[None, :], gob=g_ob[None, :], w_out=w_out.astype(BF16),
                lng=ln_g[None, :], lnb=ln_b[None, :])


def _tiles(s):
    pick = lambda pref, n: pref if n % pref == 0 else n
    bq = pick(min(FLASH_BQ_MAX, max(s // 4, LANES)), s)
    return dict(tm_in=pick(256, s), tm_up=pick(512, s), bq=bq, bk=pick(512, s), tm_out=pick(256, s))


def _encoder_layer(x, p):
    b, s, _ = x.shape
    longest = max(w // 2 for w, _ in DILATED_PATTERNS)
    assert s % longest == 0, "sequence length must be a multiple of the largest dilated chunk"
    t = _tiles(s)
    cos_b, sin_b, cos_k, sin_k, cos_t, sin_t = _rope_tables(s)
    x2 = x.reshape(b * s, D_MODEL)
    lat, qvs, kvs, vvs, ga, gb = _in_proj(x2, p["w_lat"], p["w_rest"], cos_b, sin_b, s, t["tm_in"])
    qt, k, vt = _mla_up(lat.reshape(b, s, SEG), p["gq"], p["gkv"], p["wuqt"], p["wuk"], p["wuvt"],
                        cos_k, sin_k, cos_t, sin_t, t["tm_up"])
    oa = _mla_flash(qt, k, vt, t["bq"], t["bk"]).reshape(b * s, A_WIDTH)
    obs, lses = [], []
    for i, (window, dil) in enumerate(DILATED_PATTERNS):
        assert window // (2 * dil) == HALF_WIN
        o, lse = _dilated(qvs[i], kvs[i], vvs[i], b, dil, min(DIL_BQ, s // dil))
        obs.append(o)
        lses.append(lse)
    y = _out_proj(oa, obs, lses, ga, gb, x2, p["w_out"], p["goa"], p["gob"], p["lng"], p["lnb"], t["tm_out"])
    return y.reshape(b, s, D_MODEL)


def kernel(x_prompt, x_sample, w_in, g_qn, w_uq, g_kvn, w_ukv, g_oa, g_ob, w_out, ln_g, ln_b):
    layers = [_prep_weights(w_in[l], g_qn[l], w_uq[l], g_kvn[l], w_ukv[l], g_oa[l], g_ob[l],
                            w_out[l], ln_g[l], ln_b[l]) for l in range(DEPTH)]

    def trunk(x):
        for p in layers:
            x = _encoder_layer(x, p)
        return x

    return (trunk(x_prompt), trunk(x_sample))
```

```python
import functools

import jax
import jax.numpy as jnp
from jax import lax
from jax.experimental import pallas as pl
from jax.experimental.pallas import tpu as pltpu

F32 = jnp.float32
BF16 = jnp.bfloat16

D_MODEL = 2048
DEPTH = 1
A_HEADS = 8
A_NOPE = 128
A_ROPE = 64
A_VDIM = 128
A_QK = A_NOPE + A_ROPE
A_WIDTH = A_HEADS * A_VDIM
A_VROWS = A_VDIM + 16
Q_LORA = 512
KV_LORA = 256
B_HEADS = 8
B_HDIM = 128
B_WIDTH = B_HEADS * B_HDIM
DILATED_PATTERNS = ((128, 1), (512, 4), (2048, 16))
ROPE_THETA = 10000.0
RMS_EPS = 1e-6
LN_EPS = 1e-5
NEG_BIG = -1e30
DEEPNORM_ALPHA = (2 * DEPTH) ** 0.25
LOG2E = 1.4426950408889634
A_SCALE = A_QK ** -0.5 * LOG2E
B_SCALE = B_HDIM ** -0.5 * LOG2E

LANES = 128
SEG = 1024
N_SEG = 6
LAT_PAD = SEG - (Q_LORA + KV_LORA + A_ROPE)
HALF_WIN = 64
DIL_SUBQ = 128
DIL_BQ = 512
VMEM_LIMIT = 56 * 1024 * 1024
MIX_DTYPE = BF16


def _cparams(semantics, flags=None):
    return pltpu.CompilerParams(dimension_semantics=semantics, vmem_limit_bytes=VMEM_LIMIT, flags=flags)


def _nt_dot(a, b):
    return lax.dot_general(a, b, (((1,), (1,)), ((), ())), preferred_element_type=F32)


SEG_MODES = ("lat", "rope", "rope", "plain", "silu", "silu")
SEG_SCALES = (1.0, B_SCALE, 1.0, 1.0, 1.0, 1.0)
N_DIL_OPERANDS = sum(m in ("rope", "plain") for m in SEG_MODES)


def _in_proj_kernel(x_ref, wl_ref, w_ref, cos_ref, sin_ref, *refs):
    outs, seg_ref = list(refs[:-1]), refs[-1]
    tm = x_ref.shape[0]
    xb = x_ref[...].astype(BF16)
    cos = cos_ref[...]
    sin = sin_ref[...]
    n_dil = 0
    for j, (mode, scale) in enumerate(zip(SEG_MODES, SEG_SCALES)):
        w = wl_ref[...] if j == 0 else w_ref[:, (j - 1) * SEG:j * SEG]
        acc = jnp.dot(xb, w, preferred_element_type=F32)
        if mode == "lat":
            outs.pop(0)[...] = acc
            continue
        if mode == "silu":
            outs.pop(0)[...] = (acc * jax.nn.sigmoid(acc)).astype(BF16)
            continue
        pattern_outs = [outs.pop(0) for _ in DILATED_PATTERNS]
        for h in range(B_HEADS):
            xh = acc[:, h * B_HDIM:(h + 1) * B_HDIM]
            if mode == "rope":
                xh = (xh * cos + pltpu.roll(xh, B_HDIM // 2, 1) * sin) * scale
            seg_ref[n_dil, h] = xh
            for o_ref, (_, dil) in zip(pattern_outs, DILATED_PATTERNS):
                if dil == 1:
                    o_ref[:, h * B_HDIM:(h + 1) * B_HDIM] = xh.astype(BF16)
                    continue
                for r in range(dil):
                    col = r * SEG + h * B_HDIM
                    o_ref[:, col:col + B_HDIM] = seg_ref[n_dil, h, pl.ds(r, tm // dil, stride=dil), :].astype(BF16)
        n_dil += 1


def _in_proj(x2, w_lat, w_rest, cos_b, sin_b, s, tm):
    n = x2.shape[0]
    n_pos = s // tm
    row = lambda i: (i, 0)
    out_specs, out_shape = [], []
    for mode in SEG_MODES:
        if mode in ("lat", "silu"):
            out_specs.append(pl.BlockSpec((tm, SEG), row))
            out_shape.append(jax.ShapeDtypeStruct((n, SEG), F32 if mode == "lat" else BF16))
        else:
            out_specs += [pl.BlockSpec((tm // d, d * SEG), row) for _, d in DILATED_PATTERNS]
            out_shape += [jax.ShapeDtypeStruct((n // d, d * SEG), BF16) for _, d in DILATED_PATTERNS]
    flat = pl.pallas_call(
        _in_proj_kernel,
        grid=(n // tm,),
        in_specs=[
            pl.BlockSpec((tm, D_MODEL), row),
            pl.BlockSpec((D_MODEL, SEG), lambda i: (0, 0), pipeline_mode=pl.Buffered(1)),
            pl.BlockSpec((D_MODEL, (N_SEG - 1) * SEG), lambda i: (0, 0), pipeline_mode=pl.Buffered(1)),
            pl.BlockSpec((tm, B_HDIM), lambda i: (i % n_pos, 0)),
            pl.BlockSpec((tm, B_HDIM), lambda i: (i % n_pos, 0)),
        ],
        out_specs=out_specs,
        out_shape=out_shape,
        scratch_shapes=[pltpu.VMEM((N_DIL_OPERANDS, B_HEADS, tm, B_HDIM), F32)],
        compiler_params=_cparams(("parallel",)),
        name="in_proj",
    )(x2, w_lat, w_rest, cos_b, sin_b)
    flat = list(flat)
    result = []
    for mode in SEG_MODES:
        if mode in ("lat", "silu"):
            result.append(flat.pop(0))
        else:
            result.append(tuple(flat.pop(0) for _ in DILATED_PATTERNS))
    return result


def _rms(x, g):
    return x * lax.rsqrt(jnp.mean(x * x, axis=-1, keepdims=True) + RMS_EPS) * g


def _mla_up_kernel(lat_ref, gq_ref, gkv_ref, wuqt_ref, wuk_ref, wuvt_ref,
                   cosk_ref, sink_ref, cost_ref, sint_ref, qt_ref, k_ref, vt_ref):
    lat = lat_ref[0]
    cqn = _rms(lat[:, :Q_LORA], gq_ref[...]).astype(BF16)
    ckvn = _rms(lat[:, Q_LORA:Q_LORA + KV_LORA], gkv_ref[...]).astype(BF16)
    kr = lat[:, Q_LORA + KV_LORA:Q_LORA + KV_LORA + LANES]
    lane = lax.broadcasted_iota(jnp.int32, kr.shape, 1)
    half = A_ROPE // 2
    rot = jnp.where(lane < half, pltpu.roll(kr, LANES - half, 1), pltpu.roll(kr, half, 1))
    kpe = (kr * cosk_ref[...] + rot * sink_ref[...]).astype(BF16)
    ct = cost_ref[...]
    st = sint_ref[...]
    for h in range(A_HEADS):
        qt = _nt_dot(wuqt_ref[h], cqn)
        x1 = qt[A_NOPE:A_NOPE + half]
        x2 = qt[A_NOPE + half:]
        qt_ref[0, h, 0:A_NOPE, :] = (qt[:A_NOPE] * A_SCALE).astype(BF16)
        qt_ref[0, h, A_NOPE:A_NOPE + half, :] = ((x1 * ct - x2 * st) * A_SCALE).astype(BF16)
        qt_ref[0, h, A_NOPE + half:, :] = ((x2 * ct + x1 * st) * A_SCALE).astype(BF16)
        k_ref[0, h, :, 0:A_NOPE] = jnp.dot(ckvn, wuk_ref[h], preferred_element_type=F32).astype(BF16)
        k_ref[0, h, :, A_NOPE:] = kpe[:, :A_ROPE]
        vt_ref[0, h, 0:A_VDIM, :] = _nt_dot(wuvt_ref[h], ckvn).astype(BF16)
        vt_ref[0, h, A_VDIM:, :] = jnp.ones((A_VROWS - A_VDIM, vt_ref.shape[3]), BF16)


def _mla_up(lat3, gq, gkv, wuqt, wuk, wuvt, cosk, sink, cost, sint, tm):
    b, s, _ = lat3.shape
    full = lambda *shape: pl.BlockSpec(shape, lambda bi, si: (0,) * len(shape))
    return pl.pallas_call(
        _mla_up_kernel,
        grid=(b, s // tm),
        in_specs=[
            pl.BlockSpec((1, tm, SEG), lambda bi, si: (bi, si, 0)),
            full(1, Q_LORA), full(1, KV_LORA),
            full(A_HEADS, A_QK, Q_LORA), full(A_HEADS, KV_LORA, A_NOPE), full(A_HEADS, A_VDIM, KV_LORA),
            pl.BlockSpec((tm, LANES), lambda bi, si: (si, 0)),
            pl.BlockSpec((tm, LANES), lambda bi, si: (si, 0)),
            pl.BlockSpec((A_ROPE // 2, tm), lambda bi, si: (0, si)),
            pl.BlockSpec((A_ROPE // 2, tm), lambda bi, si: (0, si)),
        ],
        out_specs=[
            pl.BlockSpec((1, A_HEADS, A_QK, tm), lambda bi, si: (bi, 0, 0, si)),
            pl.BlockSpec((1, A_HEADS, tm, A_QK), lambda bi, si: (bi, 0, si, 0)),
            pl.BlockSpec((1, A_HEADS, A_VROWS, tm), lambda bi, si: (bi, 0, 0, si)),
        ],
        out_shape=[
            jax.ShapeDtypeStruct((b, A_HEADS, A_QK, s), BF16),
            jax.ShapeDtypeStruct((b, A_HEADS, s, A_QK), BF16),
            jax.ShapeDtypeStruct((b, A_HEADS, A_VROWS, s), BF16),
        ],
        compiler_params=_cparams(("parallel", "parallel")),
        name="mla_up",
    )(lat3, gq, gkv, wuqt, wuk, wuvt, cosk, sink, cost, sint)


FLASH_BUFS = 2
FLASH_UNROLL = 4
FLASH_FLAGS = None
FLASH_BQ_MAX = 4096
FLASH_HEADROOM = 64.0


def _mla_flash_kernel(qt_ref, k_ref, vt_ref, o_ref, s_ref, p_ref, acc_ref, *, bk, nk):
    bq = qt_ref.shape[3]

    def key_block(t):
        return k_ref[0, 0, pl.ds(pl.multiple_of(t * bk, bk), bk), :]

    def value_block(t):
        return vt_ref[0, 0, :, pl.ds(pl.multiple_of(t * bk, bk), bk)]

    def finish():
        acc = acc_ref[...]
        o_ref[0] = (acc[:A_VDIM] * (1.0 / acc[A_VDIM:A_VDIM + 1])).T.astype(o_ref.dtype)

    def scores_exp(t, slot, carry):
        m_cur, m_prev, _, excess = carry
        st = jnp.dot(key_block(t), qt_ref[0, 0], preferred_element_type=F32)
        cmax = jnp.max(st, axis=0, keepdims=True)
        p_ref[slot] = jnp.exp2((st - m_cur).astype(BF16))
        return jnp.maximum(m_cur, cmax), m_cur, jnp.exp2(m_prev - m_cur), jnp.maximum(excess, cmax - m_cur)

    def values(t, slot, alpha):
        acc_ref[...] = alpha * acc_ref[...] + jnp.dot(value_block(t), p_ref[slot], preferred_element_type=F32)

    def step(t, t_slot, carry):
        alpha_t = carry[2]
        carry = scores_exp(t + 1, (t_slot + 1) % FLASH_BUFS, carry)
        values(t, t_slot, alpha_t)
        return carry

    s0 = jnp.dot(key_block(0), qt_ref[0, 0], preferred_element_type=F32)
    s_ref[...] = s0
    m0 = jnp.max(s0, axis=0, keepdims=True)
    p_ref[0] = jnp.exp2((s_ref[...] - m0).astype(BF16))
    acc_ref[...] = jnp.zeros(acc_ref.shape, F32)
    carry = (m0, m0, jnp.ones((1, bq), F32), jnp.zeros((1, bq), F32))

    def group(j, carry):
        for c in range(FLASH_UNROLL):
            carry = step(FLASH_UNROLL * j + c, c % FLASH_BUFS, carry)
        return carry

    n_groups = (nk - 1) // FLASH_UNROLL
    carry = lax.fori_loop(0, n_groups, group, carry)
    for t in range(FLASH_UNROLL * n_groups, nk - 1):
        carry = step(t, t % FLASH_BUFS, carry)
    values(nk - 1, (nk - 1) % FLASH_BUFS, carry[2])
    finish()

    @pl.when(jnp.max(carry[3]) > FLASH_HEADROOM)
    def _():
        acc_ref[...] = jnp.zeros(acc_ref.shape, F32)

        def body(t, m):
            st = jnp.dot(key_block(t), qt_ref[0, 0], preferred_element_type=F32)
            m_new = jnp.maximum(m, jnp.max(st, axis=0, keepdims=True))
            p = jnp.exp2((st - m_new).astype(BF16))
            acc_ref[...] = jnp.exp2(m - m_new) * acc_ref[...] + jnp.dot(value_block(t), p, preferred_element_type=F32)
            return m_new

        lax.fori_loop(0, nk, body, jnp.full((1, bq), NEG_BIG, F32))
        finish()


def _mla_flash(qt, k, vt, bq, bk):
    b, h, _, s = qt.shape
    nk = s // bk
    assert nk >= 2
    kern = functools.partial(_mla_flash_kernel, bk=bk, nk=nk)
    return pl.pallas_call(
        kern,
        grid=(b, h, s // bq),
        in_specs=[
            pl.BlockSpec((1, 1, A_QK, bq), lambda bi, hi, qi: (bi, hi, 0, qi)),
            pl.BlockSpec((1, 1, s, A_QK), lambda bi, hi, qi: (bi, hi, 0, 0)),
            pl.BlockSpec((1, 1, A_VROWS, s), lambda bi, hi, qi: (bi, hi, 0, 0)),
        ],
        out_specs=pl.BlockSpec((1, bq, A_VDIM), lambda bi, hi, qi: (bi, qi, hi)),
        out_shape=jax.ShapeDtypeStruct((b, s, A_WIDTH), MIX_DTYPE),
        scratch_shapes=[pltpu.VMEM((bk, bq), F32), pltpu.VMEM((FLASH_BUFS, bk, bq), BF16),
                        pltpu.VMEM((A_VROWS, bq), F32)],
        compiler_params=_cparams(("parallel", "parallel", "arbitrary"), FLASH_FLAGS),
        name="mla_flash",
    )(qt, k, vt)


def _dilated_kernel(q_ref, kp_ref, kc_ref, kn_ref, vp_ref, vc_ref, vn_ref, o_ref, lse_ref, *, bq, nblk):
    i = pl.program_id(2)
    nkeys = DIL_SUBQ + 2 * HALF_WIN
    n_sub = bq // DIL_SUBQ
    a = lax.broadcasted_iota(jnp.int32, (DIL_SUBQ, nkeys), 0)
    c = lax.broadcasted_iota(jnp.int32, (DIL_SUBQ, nkeys), 1)
    band = (c >= a) & (c <= a + 2 * HALF_WIN)
    lane = lax.broadcasted_iota(jnp.int32, (DIL_SUBQ, LANES), 1)

    def window(p_ref, c_ref, n_ref, j, hs):
        lo = j * DIL_SUBQ - HALF_WIN
        hi = lo + nkeys
        parts = []
        if lo < 0:
            parts.append(p_ref[0, :, hs])
        parts.append(c_ref[0, max(lo, 0):min(hi, bq), hs])
        if hi > bq:
            parts.append(n_ref[0, :, hs])
        return jnp.concatenate(parts, axis=0) if len(parts) > 1 else parts[0]

    for j in range(n_sub):
        mask = band
        if j == 0:
            mask = mask & ((c >= HALF_WIN) | (i > 0))
        if j == n_sub - 1:
            mask = mask & ((c < DIL_SUBQ + HALF_WIN) | (i < nblk - 1))
        bias = jnp.where(mask, 0.0, NEG_BIG)
        rows = slice(j * DIL_SUBQ, (j + 1) * DIL_SUBQ)
        lse_all = jnp.zeros((DIL_SUBQ, LANES), F32)
        for h in range(B_HEADS):
            hs = slice(h * B_HDIM, (h + 1) * B_HDIM)
            k = window(kp_ref, kc_ref, kn_ref, j, hs)
            v = window(vp_ref, vc_ref, vn_ref, j, hs)
            sc = _nt_dot(q_ref[0, rows, hs], k) + bias
            m = jnp.max(sc, axis=1, keepdims=True)
            p = jnp.exp2(sc - m).astype(BF16)
            both = jnp.dot(p, jnp.concatenate([v, jnp.ones_like(v)], axis=1), preferred_element_type=F32)
            den = both[:, B_HDIM:]
            o_ref[0, rows, hs] = (both[:, :B_HDIM] * (1.0 / den)).astype(o_ref.dtype)
            lse_all = jnp.where(lane == h, m + jnp.log2(den), lse_all)
        lse_ref[0, rows, :] = lse_all


def _dilated(qv, kv, vv, b, dil, bq):
    m_len = qv.shape[0] // b
    nblk = m_len // bq
    sub = bq // HALF_WIN
    n_half = m_len // HALF_WIN
    view = lambda t: t.reshape(b, m_len, dil * B_WIDTH)
    cur = pl.BlockSpec((1, bq, B_WIDTH), lambda bi, r, i: (bi, i, r))
    prev = pl.BlockSpec((1, HALF_WIN, B_WIDTH), lambda bi, r, i: (bi, jnp.maximum(i * sub - 1, 0), r))
    nxt = pl.BlockSpec((1, HALF_WIN, B_WIDTH), lambda bi, r, i: (bi, jnp.minimum((i + 1) * sub, n_half - 1), r))
    kern = functools.partial(_dilated_kernel, bq=bq, nblk=nblk)
    o, lse = pl.pallas_call(
        kern,
        grid=(b, dil, nblk),
        in_specs=[cur, prev, cur, nxt, prev, cur, nxt],
        out_specs=[cur, pl.BlockSpec((1, bq, LANES), lambda bi, r, i: (bi, i, r))],
        out_shape=[jax.ShapeDtypeStruct((b, m_len, dil * B_WIDTH), MIX_DTYPE),
                   jax.ShapeDtypeStruct((b, m_len, dil * LANES), F32)],
        compiler_params=_cparams(("parallel", "parallel", "parallel")),
        name=f"dilated_d{dil}",
    )(view(qv), view(kv), view(kv), view(kv), view(vv), view(vv), view(vv))
    return o.reshape(b * m_len, dil * B_WIDTH), lse.reshape(b * m_len, dil * LANES)


OUT_GROUPS = 4


def _out_proj_kernel(oa_ref, o1_ref, o2_ref, o3_ref, l1_ref, l2_ref, l3_ref, ga_ref, gb_ref, x_ref,
                     w_ref, goa_ref, gob_ref, lng_ref, lnb_ref, y_ref, on_ref, ln_ref):
    tm = oa_ref.shape[0]
    for i, (o_ref, l_ref, (_, dil)) in enumerate(zip((o1_ref, o2_ref, o3_ref), (l1_ref, l2_ref, l3_ref),
                                                     DILATED_PATTERNS)):
        for r in range(dil):
            rows = pl.ds(r, tm // dil, stride=dil) if dil > 1 else slice(None)
            ln_ref[i, rows, :] = l_ref[:, r * LANES:(r + 1) * LANES]
            for h in range(B_HEADS):
                col = r * B_WIDTH + h * B_HDIM
                on_ref[i, h, rows, :] = o_ref[:, col:col + B_HDIM].astype(F32)
    for g in range(OUT_GROUPS):
        rows = slice(g * tm // OUT_GROUPS, (g + 1) * tm // OUT_GROUPS)
        za = (_rms(oa_ref[rows, :].astype(F32), goa_ref[...]) * ga_ref[rows, :].astype(F32)).astype(BF16)
        l1, l2, l3 = ln_ref[0, rows, :], ln_ref[1, rows, :], ln_ref[2, rows, :]
        mx = jnp.maximum(jnp.maximum(l1, l2), l3)
        e1, e2, e3 = jnp.exp2(l1 - mx), jnp.exp2(l2 - mx), jnp.exp2(l3 - mx)
        inv = 1.0 / (e1 + e2 + e3)
        w1, w2, w3 = e1 * inv, e2 * inv, e3 * inv
        parts = []
        for h in range(B_HEADS):
            parts.append(w1[:, h:h + 1] * on_ref[0, h, rows, :] + w2[:, h:h + 1] * on_ref[1, h, rows, :]
                         + w3[:, h:h + 1] * on_ref[2, h, rows, :])
        ob = jnp.concatenate(parts, axis=1)
        zb = (_rms(ob, gob_ref[...]) * gb_ref[rows, :].astype(F32)).astype(BF16)
        out = (jnp.dot(za, w_ref[:A_WIDTH, :], preferred_element_type=F32)
               + jnp.dot(zb, w_ref[A_WIDTH:, :], preferred_element_type=F32))
        r = DEEPNORM_ALPHA * x_ref[rows, :] + out
        mu = jnp.mean(r, axis=-1, keepdims=True)
        d = r - mu
        var = jnp.mean(d * d, axis=-1, keepdims=True)
        y_ref[rows, :] = d * lax.rsqrt(var + LN_EPS) * lng_ref[...] + lnb_ref[...]


def _out_proj(oa, obs, lses, ga, gb, x2, w_out, goa, gob, lng, lnb, tm):
    n = x2.shape[0]
    row = lambda w: pl.BlockSpec((tm, w), lambda i: (i, 0))
    view = lambda w: [pl.BlockSpec((tm // d, d * w), lambda i: (i, 0)) for _, d in DILATED_PATTERNS]
    full = lambda *shape: pl.BlockSpec(shape, lambda i: (0,) * len(shape))
    n_pat = len(DILATED_PATTERNS)
    return pl.pallas_call(
        _out_proj_kernel,
        grid=(n // tm,),
        in_specs=[row(A_WIDTH)] + view(B_WIDTH) + view(LANES) + [row(A_WIDTH), row(B_WIDTH), row(D_MODEL),
                  pl.BlockSpec((A_WIDTH + B_WIDTH, D_MODEL), lambda i: (0, 0), pipeline_mode=pl.Buffered(1)), full(1, A_WIDTH), full(1, B_WIDTH),
                  full(1, D_MODEL), full(1, D_MODEL)],
        out_specs=row(D_MODEL),
        out_shape=jax.ShapeDtypeStruct((n, D_MODEL), F32),
        scratch_shapes=[pltpu.VMEM((n_pat, B_HEADS, tm, B_HDIM), F32), pltpu.VMEM((n_pat, tm, LANES), F32)],
        compiler_params=_cparams(("parallel",)),
        name="out_proj",
    )(oa, *obs, *lses, ga, gb, x2, w_out, goa, gob, lng, lnb)


def _rope_tables(s):
    pos = jnp.arange(s, dtype=F32)

    def cos_sin(d):
        half = d // 2
        inv = ROPE_THETA ** (-jnp.arange(half, dtype=F32) * 2.0 / d)
        ang = pos[:, None] * inv[None, :]
        return jnp.cos(ang), jnp.sin(ang)

    cb, sb = cos_sin(B_HDIM)
    cos_b = jnp.concatenate([cb, cb], axis=1)
    sin_b = jnp.concatenate([-sb, sb], axis=1)
    ca, sa = cos_sin(A_ROPE)
    zeros = jnp.zeros((s, LANES - A_ROPE), F32)
    cos_k = jnp.concatenate([ca, ca, zeros], axis=1)
    sin_k = jnp.concatenate([-sa, sa, zeros], axis=1)
    return cos_b, sin_b, cos_k, sin_k, ca.T, sa.T


def _prep_weights(w_in, g_qn, w_uq, g_kvn, w_ukv, g_oa, g_ob, w_out, ln_g, ln_b):
    n_lat = Q_LORA + KV_LORA + A_ROPE
    w_lat = jnp.pad(w_in[:, :n_lat], ((0, 0), (0, LAT_PAD))).astype(BF16)
    w_rest = w_in[:, n_lat:].astype(BF16)
    wuqt = w_uq.reshape(Q_LORA, A_HEADS, A_QK).transpose(1, 2, 0).astype(BF16)
    wukv = w_ukv.reshape(KV_LORA, A_HEADS, A_NOPE + A_VDIM)
    wuk = wukv[:, :, :A_NOPE].transpose(1, 0, 2).astype(BF16)
    wuvt = wukv[:, :, A_NOPE:].transpose(1, 2, 0).astype(BF16)
    return dict(w_lat=w_lat, w_rest=w_rest, gq=g_qn[None, :], gkv=g_kvn[None, :], wuqt=wuqt, wuk=wuk, wuvt=wuvt,
                goa=g_oa[None, :], gob=g_ob[None, :], w_out=w_out.astype(BF16),
                lng=ln_g[None, :], lnb=ln_b[None, :])


def _tiles(s):
    pick = lambda pref, n: pref if n % pref == 0 else n
    bq = pick(min(FLASH_BQ_MAX, max(s // 4, LANES)), s)
    return dict(tm_in=pick(256, s), tm_up=pick(512, s), bq=bq, bk=pick(512, s), tm_out=pick(512, s))


def _encoder_layer(x, p):
    b, s, _ = x.shape
    longest = max(w // 2 for w, _ in DILATED_PATTERNS)
    assert s % longest == 0, "sequence length must be a multiple of the largest dilated chunk"
    t = _tiles(s)
    cos_b, sin_b, cos_k, sin_k, cos_t, sin_t = _rope_tables(s)
    x2 = x.reshape(b * s, D_MODEL)
    lat, qvs, kvs, vvs, ga, gb = _in_proj(x2, p["w_lat"], p["w_rest"], cos_b, sin_b, s, t["tm_in"])
    qt, k, vt = _mla_up(lat.reshape(b, s, SEG), p["gq"], p["gkv"], p["wuqt"], p["wuk"], p["wuvt"],
                        cos_k, sin_k, cos_t, sin_t, t["tm_up"])
    oa = _mla_flash(qt, k, vt, t["bq"], t["bk"]).reshape(b * s, A_WIDTH)
    obs, lses = [], []
    for i, (window, dil) in enumerate(DILATED_PATTERNS):
        assert window // (2 * dil) == HALF_WIN
        o, lse = _dilated(qvs[i], kvs[i], vvs[i], b, dil, min(DIL_BQ, s // dil))
        obs.append(o)
        lses.append(lse)
    y = _out_proj(oa, obs, lses, ga, gb, x2, p["w_out"], p["goa"], p["gob"], p["lng"], p["lnb"], t["tm_out"])
    return y.reshape(b, s, D_MODEL)


def kernel(x_prompt, x_sample, w_in, g_qn, w_uq, g_kvn, w_ukv, g_oa, g_ob, w_out, ln_g, ln_b):
    layers = [_prep_weights(w_in[l], g_qn[l], w_uq[l], g_kvn[l], w_ukv[l], g_oa[l], g_ob[l],
                            w_out[l], ln_g[l], ln_b[l]) for l in range(DEPTH)]

    def trunk(x):
        for p in layers:
            x = _encoder_layer(x, p)
        return x

    return (trunk(x_prompt), trunk(x_sample))
```

```python
import functools

import jax
import jax.numpy as jnp
from jax import lax
from jax.experimental import pallas as pl
from jax.experimental.pallas import tpu as pltpu

F32 = jnp.float32
BF16 = jnp.bfloat16

D_MODEL = 2048
DEPTH = 1
A_HEADS = 8
A_NOPE = 128
A_ROPE = 64
A_VDIM = 128
A_QK = A_NOPE + A_ROPE
A_WIDTH = A_HEADS * A_VDIM
BF16_SUBLANES = 16
A_VROWS = A_VDIM + BF16_SUBLANES
Q_LORA = 512
KV_LORA = 256
B_HEADS = 8
B_HDIM = 128
B_WIDTH = B_HEADS * B_HDIM
DILATED_PATTERNS = ((128, 1), (512, 4), (2048, 16))
ROPE_THETA = 10000.0
RMS_EPS = 1e-6
LN_EPS = 1e-5
NEG_BIG = -1e30
DEEPNORM_ALPHA = (2 * DEPTH) ** 0.25
LOG2E = 1.4426950408889634
A_SCALE = A_QK ** -0.5 * LOG2E
B_SCALE = B_HDIM ** -0.5 * LOG2E

LANES = 128
SEG = 1024
LAT_PAD = SEG - (Q_LORA + KV_LORA + A_ROPE)
HALF_WIN = 64
DIL_SUBQ = 128
DIL_BQ = 512
VMEM_LIMIT = 56 * 1024 * 1024
MIX_DTYPE = BF16


def _cparams(semantics):
    return pltpu.CompilerParams(dimension_semantics=semantics, vmem_limit_bytes=VMEM_LIMIT)


def _nt_dot(a, b):
    return lax.dot_general(a, b, (((1,), (1,)), ((), ())), preferred_element_type=F32)


SEG_MODES = ("lat", "rope", "rope", "plain", "silu", "silu")
N_SEG = len(SEG_MODES)
SEG_SCALES = (1.0, B_SCALE, 1.0, 1.0, 1.0, 1.0)
N_DIL_OPERANDS = sum(m in ("rope", "plain") for m in SEG_MODES)


def _in_proj_kernel(x_ref, wl_ref, w_ref, cos_ref, sin_ref, *refs):
    outs, seg_ref = list(refs[:-1]), refs[-1]
    tm = x_ref.shape[0]
    xb = x_ref[...].astype(BF16)
    cos = cos_ref[...]
    sin = sin_ref[...]
    n_dil = 0
    for j, (mode, scale) in enumerate(zip(SEG_MODES, SEG_SCALES)):
        w = wl_ref[...] if j == 0 else w_ref[:, (j - 1) * SEG:j * SEG]
        acc = jnp.dot(xb, w, preferred_element_type=F32)
        if mode == "lat":
            outs.pop(0)[...] = acc
            continue
        if mode == "silu":
            outs.pop(0)[...] = (acc * jax.nn.sigmoid(acc)).astype(BF16)
            continue
        pattern_outs = [outs.pop(0) for _ in DILATED_PATTERNS]
        for h in range(B_HEADS):
            xh = acc[:, h * B_HDIM:(h + 1) * B_HDIM]
            if mode == "rope":
                xh = (xh * cos + pltpu.roll(xh, B_HDIM // 2, 1) * sin) * scale
            seg_ref[n_dil, h] = xh
            for o_ref, (_, dil) in zip(pattern_outs, DILATED_PATTERNS):
                if dil == 1:
                    o_ref[:, h * B_HDIM:(h + 1) * B_HDIM] = xh.astype(BF16)
                    continue
                for r in range(dil):
                    col = r * SEG + h * B_HDIM
                    o_ref[:, col:col + B_HDIM] = seg_ref[n_dil, h, pl.ds(r, tm // dil, stride=dil), :].astype(BF16)
        n_dil += 1


def _in_proj(x2, w_lat, w_rest, cos_b, sin_b, s, tm):
    n = x2.shape[0]
    n_pos = s // tm
    row = lambda i: (i, 0)
    out_specs, out_shape = [], []
    for mode in SEG_MODES:
        if mode in ("lat", "silu"):
            out_specs.append(pl.BlockSpec((tm, SEG), row))
            out_shape.append(jax.ShapeDtypeStruct((n, SEG), F32 if mode == "lat" else BF16))
        else:
            out_specs += [pl.BlockSpec((tm // d, d * SEG), row) for _, d in DILATED_PATTERNS]
            out_shape += [jax.ShapeDtypeStruct((n // d, d * SEG), BF16) for _, d in DILATED_PATTERNS]
    flat = pl.pallas_call(
        _in_proj_kernel,
        grid=(n // tm,),
        in_specs=[
            pl.BlockSpec((tm, D_MODEL), row),
            pl.BlockSpec((D_MODEL, SEG), lambda i: (0, 0), pipeline_mode=pl.Buffered(1)),
            pl.BlockSpec((D_MODEL, (N_SEG - 1) * SEG), lambda i: (0, 0), pipeline_mode=pl.Buffered(1)),
            pl.BlockSpec((tm, B_HDIM), lambda i: (i % n_pos, 0)),
            pl.BlockSpec((tm, B_HDIM), lambda i: (i % n_pos, 0)),
        ],
        out_specs=out_specs,
        out_shape=out_shape,
        scratch_shapes=[pltpu.VMEM((N_DIL_OPERANDS, B_HEADS, tm, B_HDIM), F32)],
        compiler_params=_cparams(("parallel",)),
        name="in_proj",
    )(x2, w_lat, w_rest, cos_b, sin_b)
    flat = list(flat)
    result = []
    for mode in SEG_MODES:
        if mode in ("lat", "silu"):
            result.append(flat.pop(0))
        else:
            result.append(tuple(flat.pop(0) for _ in DILATED_PATTERNS))
    return result


def _rms(x, g):
    return x * lax.rsqrt(jnp.mean(x * x, axis=-1, keepdims=True) + RMS_EPS) * g


def _mla_up_kernel(lat_ref, gq_ref, gkv_ref, wuqt_ref, wuk_ref, wuvt_ref,
                   cosk_ref, sink_ref, cost_ref, sint_ref, qt_ref, k_ref, vt_ref):
    lat = lat_ref[0]
    cqn = _rms(lat[:, :Q_LORA], gq_ref[...]).astype(BF16)
    ckvn = _rms(lat[:, Q_LORA:Q_LORA + KV_LORA], gkv_ref[...]).astype(BF16)
    kr = lat[:, Q_LORA + KV_LORA:Q_LORA + KV_LORA + LANES]
    lane = lax.broadcasted_iota(jnp.int32, kr.shape, 1)
    half = A_ROPE // 2
    rot = jnp.where(lane < half, pltpu.roll(kr, LANES - half, 1), pltpu.roll(kr, half, 1))
    kpe = (kr * cosk_ref[...] + rot * sink_ref[...]).astype(BF16)
    ct = cost_ref[...]
    st = sint_ref[...]
    for h in range(A_HEADS):
        qt = _nt_dot(wuqt_ref[h], cqn)
        x1 = qt[A_NOPE:A_NOPE + half]
        x2 = qt[A_NOPE + half:]
        qt_ref[0, h, 0:A_NOPE, :] = (qt[:A_NOPE] * A_SCALE).astype(BF16)
        qt_ref[0, h, A_NOPE:A_NOPE + half, :] = ((x1 * ct - x2 * st) * A_SCALE).astype(BF16)
        qt_ref[0, h, A_NOPE + half:, :] = ((x2 * ct + x1 * st) * A_SCALE).astype(BF16)
        k_ref[0, h, :, 0:A_NOPE] = jnp.dot(ckvn, wuk_ref[h], preferred_element_type=F32).astype(BF16)
        k_ref[0, h, :, A_NOPE:] = kpe[:, :A_ROPE]
        vt_ref[0, h, 0:A_VDIM, :] = _nt_dot(wuvt_ref[h], ckvn).astype(BF16)
        vt_ref[0, h, A_VDIM:, :] = jnp.ones((A_VROWS - A_VDIM, vt_ref.shape[3]), BF16)


def _mla_up(lat3, gq, gkv, wuqt, wuk, wuvt, cosk, sink, cost, sint, tm):
    b, s, _ = lat3.shape
    full = lambda *shape: pl.BlockSpec(shape, lambda bi, si: (0,) * len(shape))
    return pl.pallas_call(
        _mla_up_kernel,
        grid=(b, s // tm),
        in_specs=[
            pl.BlockSpec((1, tm, SEG), lambda bi, si: (bi, si, 0)),
            full(1, Q_LORA), full(1, KV_LORA),
            full(A_HEADS, A_QK, Q_LORA), full(A_HEADS, KV_LORA, A_NOPE), full(A_HEADS, A_VDIM, KV_LORA),
            pl.BlockSpec((tm, LANES), lambda bi, si: (si, 0)),
            pl.BlockSpec((tm, LANES), lambda bi, si: (si, 0)),
            pl.BlockSpec((A_ROPE // 2, tm), lambda bi, si: (0, si)),
            pl.BlockSpec((A_ROPE // 2, tm), lambda bi, si: (0, si)),
        ],
        out_specs=[
            pl.BlockSpec((1, A_HEADS, A_QK, tm), lambda bi, si: (bi, 0, 0, si)),
            pl.BlockSpec((1, A_HEADS, tm, A_QK), lambda bi, si: (bi, 0, si, 0)),
            pl.BlockSpec((1, A_HEADS, A_VROWS, tm), lambda bi, si: (bi, 0, 0, si)),
        ],
        out_shape=[
            jax.ShapeDtypeStruct((b, A_HEADS, A_QK, s), BF16),
            jax.ShapeDtypeStruct((b, A_HEADS, s, A_QK), BF16),
            jax.ShapeDtypeStruct((b, A_HEADS, A_VROWS, s), BF16),
        ],
        compiler_params=_cparams(("parallel", "parallel")),
        name="mla_up",
    )(lat3, gq, gkv, wuqt, wuk, wuvt, cosk, sink, cost, sint)


FLASH_BUFS = 2
FLASH_UNROLL = 4
FLASH_BQ_MAX = 4096
FLASH_HEADROOM = 64.0


def _mla_flash_kernel(qt_ref, k_ref, vt_ref, o_ref, s_ref, p_ref, acc_ref, *, bk, nk):
    bq = qt_ref.shape[3]

    def key_block(t):
        return k_ref[0, 0, pl.ds(pl.multiple_of(t * bk, bk), bk), :]

    def value_block(t):
        return vt_ref[0, 0, :, pl.ds(pl.multiple_of(t * bk, bk), bk)]

    def finish():
        acc = acc_ref[...]
        o_ref[0] = (acc[:A_VDIM] * (1.0 / acc[A_VDIM:A_VDIM + 1])).T.astype(o_ref.dtype)

    def scores_exp(t, slot, carry):
        m_cur, m_prev, _, excess = carry
        st = jnp.dot(key_block(t), qt_ref[0, 0], preferred_element_type=F32)
        cmax = jnp.max(st, axis=0, keepdims=True)
        p_ref[slot] = jnp.exp2((st - m_cur).astype(BF16))
        return jnp.maximum(m_cur, cmax), m_cur, jnp.exp2(m_prev - m_cur), jnp.maximum(excess, cmax - m_cur)

    def values(t, slot, alpha):
        acc_ref[...] = alpha * acc_ref[...] + jnp.dot(value_block(t), p_ref[slot], preferred_element_type=F32)

    def step(t, t_slot, carry):
        alpha_t = carry[2]
        carry = scores_exp(t + 1, (t_slot + 1) % FLASH_BUFS, carry)
        values(t, t_slot, alpha_t)
        return carry

    s0 = jnp.dot(key_block(0), qt_ref[0, 0], preferred_element_type=F32)
    s_ref[...] = s0
    m0 = jnp.max(s0, axis=0, keepdims=True)
    p_ref[0] = jnp.exp2((s_ref[...] - m0).astype(BF16))
    acc_ref[...] = jnp.zeros(acc_ref.shape, F32)
    carry = (m0, m0, jnp.ones((1, bq), F32), jnp.zeros((1, bq), F32))

    def group(j, carry):
        for c in range(FLASH_UNROLL):
            carry = step(FLASH_UNROLL * j + c, c % FLASH_BUFS, carry)
        return carry

    n_groups = (nk - 1) // FLASH_UNROLL
    carry = lax.fori_loop(0, n_groups, group, carry)
    for t in range(FLASH_UNROLL * n_groups, nk - 1):
        carry = step(t, t % FLASH_BUFS, carry)
    values(nk - 1, (nk - 1) % FLASH_BUFS, carry[2])
    finish()

    @pl.when(jnp.max(carry[3]) > FLASH_HEADROOM)
    def _():
        acc_ref[...] = jnp.zeros(acc_ref.shape, F32)

        def body(t, m):
            st = jnp.dot(key_block(t), qt_ref[0, 0], preferred_element_type=F32)
            m_new = jnp.maximum(m, jnp.max(st, axis=0, keepdims=True))
            p = jnp.exp2((st - m_new).astype(BF16))
            acc_ref[...] = jnp.exp2(m - m_new) * acc_ref[...] + jnp.dot(value_block(t), p, preferred_element_type=F32)
            return m_new

        lax.fori_loop(0, nk, body, jnp.full((1, bq), NEG_BIG, F32))
        finish()


def _mla_flash(qt, k, vt, bq, bk):
    b, h, _, s = qt.shape
    nk = s // bk
    assert nk >= 2
    kern = functools.partial(_mla_flash_kernel, bk=bk, nk=nk)
    return pl.pallas_call(
        kern,
        grid=(b, h, s // bq),
        in_specs=[
            pl.BlockSpec((1, 1, A_QK, bq), lambda bi, hi, qi: (bi, hi, 0, qi)),
            pl.BlockSpec((1, 1, s, A_QK), lambda bi, hi, qi: (bi, hi, 0, 0)),
            pl.BlockSpec((1, 1, A_VROWS, s), lambda bi, hi, qi: (bi, hi, 0, 0)),
        ],
        out_specs=pl.BlockSpec((1, bq, A_VDIM), lambda bi, hi, qi: (bi, qi, hi)),
        out_shape=jax.ShapeDtypeStruct((b, s, A_WIDTH), MIX_DTYPE),
        scratch_shapes=[pltpu.VMEM((bk, bq), F32), pltpu.VMEM((FLASH_BUFS, bk, bq), BF16),
                        pltpu.VMEM((A_VROWS, bq), F32)],
        compiler_params=_cparams(("parallel", "parallel", "arbitrary")),
        name="mla_flash",
    )(qt, k, vt)


def _dilated_kernel(q_ref, kp_ref, kc_ref, kn_ref, vp_ref, vc_ref, vn_ref, o_ref, lse_ref, *, bq, nblk):
    i = pl.program_id(2)
    nkeys = DIL_SUBQ + 2 * HALF_WIN
    n_sub = bq // DIL_SUBQ
    a = lax.broadcasted_iota(jnp.int32, (DIL_SUBQ, nkeys), 0)
    c = lax.broadcasted_iota(jnp.int32, (DIL_SUBQ, nkeys), 1)
    band = (c >= a) & (c <= a + 2 * HALF_WIN)
    lane = lax.broadcasted_iota(jnp.int32, (DIL_SUBQ, LANES), 1)

    def window(p_ref, c_ref, n_ref, j, hs):
        lo = j * DIL_SUBQ - HALF_WIN
        hi = lo + nkeys
        parts = []
        if lo < 0:
            parts.append(p_ref[0, :, hs])
        parts.append(c_ref[0, max(lo, 0):min(hi, bq), hs])
        if hi > bq:
            parts.append(n_ref[0, :, hs])
        return jnp.concatenate(parts, axis=0) if len(parts) > 1 else parts[0]

    for j in range(n_sub):
        mask = band
        if j == 0:
            mask = mask & ((c >= HALF_WIN) | (i > 0))
        if j == n_sub - 1:
            mask = mask & ((c < DIL_SUBQ + HALF_WIN) | (i < nblk - 1))
        bias = jnp.where(mask, 0.0, NEG_BIG)
        rows = slice(j * DIL_SUBQ, (j + 1) * DIL_SUBQ)
        lse_all = jnp.zeros((DIL_SUBQ, LANES), F32)
        for h in range(B_HEADS):
            hs = slice(h * B_HDIM, (h + 1) * B_HDIM)
            k = window(kp_ref, kc_ref, kn_ref, j, hs)
            v = window(vp_ref, vc_ref, vn_ref, j, hs)
            sc = _nt_dot(q_ref[0, rows, hs], k) + bias
            m = jnp.max(sc, axis=1, keepdims=True)
            p = jnp.exp2(sc - m).astype(BF16)
            both = jnp.dot(p, jnp.concatenate([v, jnp.ones_like(v)], axis=1), preferred_element_type=F32)
            den = both[:, B_HDIM:]
            o_ref[0, rows, hs] = (both[:, :B_HDIM] * (1.0 / den)).astype(o_ref.dtype)
            lse_all = jnp.where(lane == h, m + jnp.log2(den), lse_all)
        lse_ref[0, rows, :] = lse_all


def _dilated(qv, kv, vv, b, dil, bq):
    m_len = qv.shape[0] // b
    nblk = m_len // bq
    sub = bq // HALF_WIN
    n_half = m_len // HALF_WIN
    view = lambda t: t.reshape(b, m_len, dil * B_WIDTH)
    cur = pl.BlockSpec((1, bq, B_WIDTH), lambda bi, r, i: (bi, i, r))
    prev = pl.BlockSpec((1, HALF_WIN, B_WIDTH), lambda bi, r, i: (bi, jnp.maximum(i * sub - 1, 0), r))
    nxt = pl.BlockSpec((1, HALF_WIN, B_WIDTH), lambda bi, r, i: (bi, jnp.minimum((i + 1) * sub, n_half - 1), r))
    kern = functools.partial(_dilated_kernel, bq=bq, nblk=nblk)
    o, lse = pl.pallas_call(
        kern,
        grid=(b, dil, nblk),
        in_specs=[cur, prev, cur, nxt, prev, cur, nxt],
        out_specs=[cur, pl.BlockSpec((1, bq, LANES), lambda bi, r, i: (bi, i, r))],
        out_shape=[jax.ShapeDtypeStruct((b, m_len, dil * B_WIDTH), MIX_DTYPE),
                   jax.ShapeDtypeStruct((b, m_len, dil * LANES), F32)],
        compiler_params=_cparams(("parallel", "parallel", "parallel")),
        name=f"dilated_d{dil}",
    )(view(qv), view(kv), view(kv), view(kv), view(vv), view(vv), view(vv))
    return o.reshape(b * m_len, dil * B_WIDTH), lse.reshape(b * m_len, dil * LANES)


OUT_GROUPS = 4


def _out_proj_kernel(oa_ref, o1_ref, o2_ref, o3_ref, l1_ref, l2_ref, l3_ref, ga_ref, gb_ref, x_ref,
                     w_ref, goa_ref, gob_ref, lng_ref, lnb_ref, y_ref, on_ref, ln_ref):
    tm = oa_ref.shape[0]
    for i, (o_ref, l_ref, (_, dil)) in enumerate(zip((o1_ref, o2_ref, o3_ref), (l1_ref, l2_ref, l3_ref),
                                                     DILATED_PATTERNS)):
        for r in range(dil):
            rows = pl.ds(r, tm // dil, stride=dil) if dil > 1 else slice(None)
            ln_ref[i, rows, :] = l_ref[:, r * LANES:(r + 1) * LANES]
            for h in range(B_HEADS):
                col = r * B_WIDTH + h * B_HDIM
                on_ref[i, h, rows, :] = o_ref[:, col:col + B_HDIM].astype(F32)
    for g in range(OUT_GROUPS):
        rows = slice(g * tm // OUT_GROUPS, (g + 1) * tm // OUT_GROUPS)
        za = (_rms(oa_ref[rows, :].astype(F32), goa_ref[...]) * ga_ref[rows, :].astype(F32)).astype(BF16)
        l1, l2, l3 = ln_ref[0, rows, :], ln_ref[1, rows, :], ln_ref[2, rows, :]
        mx = jnp.maximum(jnp.maximum(l1, l2), l3)
        e1, e2, e3 = jnp.exp2(l1 - mx), jnp.exp2(l2 - mx), jnp.exp2(l3 - mx)
        inv = 1.0 / (e1 + e2 + e3)
        w1, w2, w3 = e1 * inv, e2 * inv, e3 * inv
        parts = []
        for h in range(B_HEADS):
            parts.append(w1[:, h:h + 1] * on_ref[0, h, rows, :] + w2[:, h:h + 1] * on_ref[1, h, rows, :]
                         + w3[:, h:h + 1] * on_ref[2, h, rows, :])
        ob = jnp.concatenate(parts, axis=1)
        zb = (_rms(ob, gob_ref[...]) * gb_ref[rows, :].astype(F32)).astype(BF16)
        out = (jnp.dot(za, w_ref[:A_WIDTH, :], preferred_element_type=F32)
               + jnp.dot(zb, w_ref[A_WIDTH:, :], preferred_element_type=F32))
        r = DEEPNORM_ALPHA * x_ref[rows, :] + out
        mu = jnp.mean(r, axis=-1, keepdims=True)
        d = r - mu
        var = jnp.mean(d * d, axis=-1, keepdims=True)
        y_ref[rows, :] = d * lax.rsqrt(var + LN_EPS) * lng_ref[...] + lnb_ref[...]


def _out_proj(oa, obs, lses, ga, gb, x2, w_out, goa, gob, lng, lnb, tm):
    n = x2.shape[0]
    row = lambda w: pl.BlockSpec((tm, w), lambda i: (i, 0))
    view = lambda w: [pl.BlockSpec((tm // d, d * w), lambda i: (i, 0)) for _, d in DILATED_PATTERNS]
    full = lambda *shape: pl.BlockSpec(shape, lambda i: (0,) * len(shape))
    n_pat = len(DILATED_PATTERNS)
    return pl.pallas_call(
        _out_proj_kernel,
        grid=(n // tm,),
        in_specs=[row(A_WIDTH)] + view(B_WIDTH) + view(LANES) + [row(A_WIDTH), row(B_WIDTH), row(D_MODEL),
                  pl.BlockSpec((A_WIDTH + B_WIDTH, D_MODEL), lambda i: (0, 0), pipeline_mode=pl.Buffered(1)),
                  full(1, A_WIDTH), full(1, B_WIDTH),
                  full(1, D_MODEL), full(1, D_MODEL)],
        out_specs=row(D_MODEL),
        out_shape=jax.ShapeDtypeStruct((n, D_MODEL), F32),
        scratch_shapes=[pltpu.VMEM((n_pat, B_HEADS, tm, B_HDIM), F32), pltpu.VMEM((n_pat, tm, LANES), F32)],
        compiler_params=_cparams(("parallel",)),
        name="out_proj",
    )(oa, *obs, *lses, ga, gb, x2, w_out, goa, gob, lng, lnb)


def _rope_tables(s):
    pos = jnp.arange(s, dtype=F32)

    def cos_sin(d):
        half = d // 2
        inv = ROPE_THETA ** (-jnp.arange(half, dtype=F32) * 2.0 / d)
        ang = pos[:, None] * inv[None, :]
        return jnp.cos(ang), jnp.sin(ang)

    cb, sb = cos_sin(B_HDIM)
    cos_b = jnp.concatenate([cb, cb], axis=1)
    sin_b = jnp.concatenate([-sb, sb], axis=1)
    ca, sa = cos_sin(A_ROPE)
    zeros = jnp.zeros((s, LANES - A_ROPE), F32)
    cos_k = jnp.concatenate([ca, ca, zeros], axis=1)
    sin_k = jnp.concatenate([-sa, sa, zeros], axis=1)
    return cos_b, sin_b, cos_k, sin_k, ca.T, sa.T


def _prep_weights(w_in, g_qn, w_uq, g_kvn, w_ukv, g_oa, g_ob, w_out, ln_g, ln_b):
    n_lat = Q_LORA + KV_LORA + A_ROPE
    w_lat = jnp.pad(w_in[:, :n_lat], ((0, 0), (0, LAT_PAD))).astype(BF16)
    w_rest = w_in[:, n_lat:].astype(BF16)
    wuqt = w_uq.reshape(Q_LORA, A_HEADS, A_QK).transpose(1, 2, 0).astype(BF16)
    wukv = w_ukv.reshape(KV_LORA, A_HEADS, A_NOPE + A_VDIM)
    wuk = wukv[:, :, :A_NOPE].transpose(1, 0, 2).astype(BF16)
    wuvt = wukv[:, :, A_NOPE:].transpose(1, 2, 0).astype(BF16)
    return dict(w_lat=w_lat, w_rest=w_rest, gq=g_qn[None, :], gkv=g_kvn[None, :], wuqt=wuqt, wuk=wuk, wuvt=wuvt,
                goa=g_oa[None, :], gob=g_ob[None, :], w_out=w_out.astype(BF16),
                lng=ln_g[None, :], lnb=ln_b[None, :])


def _tiles(s):
    pick = lambda pref, n: pref if n % pref == 0 else n
    bq = pick(min(FLASH_BQ_MAX, max(s // 4, LANES)), s)
    return dict(tm_in=pick(256, s), tm_up=pick(512, s), bq=bq, bk=pick(512, s), tm_out=pick(512, s))


def _encoder_layer(x, p):
    b, s, _ = x.shape
    longest = max(w // 2 for w, _ in DILATED_PATTERNS)
    assert s % longest == 0, "sequence length must be a multiple of the largest dilated chunk"
    t = _tiles(s)
    cos_b, sin_b, cos_k, sin_k, cos_t, sin_t = _rope_tables(s)
    x2 = x.reshape(b * s, D_MODEL)
    lat, qvs, kvs, vvs, ga, gb = _in_proj(x2, p["w_lat"], p["w_rest"], cos_b, sin_b, s, t["tm_in"])
    qt, k, vt = _mla_up(lat.reshape(b, s, SEG), p["gq"], p["gkv"], p["wuqt"], p["wuk"], p["wuvt"],
                        cos_k, sin_k, cos_t, sin_t, t["tm_up"])
    oa = _mla_flash(qt, k, vt, t["bq"], t["bk"]).reshape(b * s, A_WIDTH)
    obs, lses = [], []
    for i, (window, dil) in enumerate(DILATED_PATTERNS):
        assert window // (2 * dil) == HALF_WIN
        o, lse = _dilated(qvs[i], kvs[i], vvs[i], b, dil, min(DIL_BQ, s // dil))
        obs.append(o)
        lses.append(lse)
    y = _out_proj(oa, obs, lses, ga, gb, x2, p["w_out"], p["goa"], p["gob"], p["lng"], p["lnb"], t["tm_out"])
    return y.reshape(b, s, D_MODEL)


def kernel(x_prompt, x_sample, w_in, g_qn, w_uq, g_kvn, w_ukv, g_oa, g_ob, w_out, ln_g, ln_b):
    layers = [_prep_weights(w_in[l], g_qn[l], w_uq[l], g_kvn[l], w_ukv[l], g_oa[l], g_ob[l],
                            w_out[l], ln_g[l], ln_b[l]) for l in range(DEPTH)]

    def trunk(x):
        for p in layers:
            x = _encoder_layer(x, p)
        return x

    return (trunk(x_prompt), trunk(x_sample))
```

```python
import functools

import jax
import jax.numpy as jnp
from jax import lax
from jax.experimental import pallas as pl
from jax.experimental.pallas import tpu as pltpu

F32 = jnp.float32
BF16 = jnp.bfloat16

D_MODEL = 2048
DEPTH = 1
A_HEADS = 8
A_NOPE = 128
A_ROPE = 64
A_VDIM = 128
A_QK = A_NOPE + A_ROPE
A_WIDTH = A_HEADS * A_VDIM
BF16_SUBLANES = 16
A_VROWS = A_VDIM + BF16_SUBLANES
Q_LORA = 512
KV_LORA = 256
B_HEADS = 8
B_HDIM = 128
B_WIDTH = B_HEADS * B_HDIM
DILATED_PATTERNS = ((128, 1), (512, 4), (2048, 16))
ROPE_THETA = 10000.0
RMS_EPS = 1e-6
LN_EPS = 1e-5
NEG_BIG = -1e30
DEEPNORM_ALPHA = (2 * DEPTH) ** 0.25
LOG2E = 1.4426950408889634
A_SCALE = A_QK ** -0.5 * LOG2E
B_SCALE = B_HDIM ** -0.5 * LOG2E

LANES = 128
SEG = 1024
LAT_PAD = SEG - (Q_LORA + KV_LORA + A_ROPE)
HALF_WIN = 64
DIL_SUBQ = 128
DIL_BQ = 1024
VMEM_LIMIT = 56 * 1024 * 1024
MIX_DTYPE = BF16


def _cparams(semantics):
    return pltpu.CompilerParams(dimension_semantics=semantics, vmem_limit_bytes=VMEM_LIMIT)


def _nt_dot(a, b):
    return lax.dot_general(a, b, (((1,), (1,)), ((), ())), preferred_element_type=F32)


SEG_MODES = ("lat", "rope", "rope", "plain", "silu", "silu")
N_SEG = len(SEG_MODES)
SEG_SCALES = (1.0, B_SCALE, 1.0, 1.0, 1.0, 1.0)
N_DIL_OPERANDS = sum(m in ("rope", "plain") for m in SEG_MODES)


def _in_proj_kernel(x_ref, wl_ref, w_ref, cos_ref, sin_ref, *refs):
    outs, seg_ref = list(refs[:-1]), refs[-1]
    tm = x_ref.shape[0]
    xb = x_ref[...].astype(BF16)
    cos = cos_ref[...]
    sin = sin_ref[...]
    n_dil = 0
    for j, (mode, scale) in enumerate(zip(SEG_MODES, SEG_SCALES)):
        w = wl_ref[...] if j == 0 else w_ref[:, (j - 1) * SEG:j * SEG]
        acc = jnp.dot(xb, w, preferred_element_type=F32)
        if mode == "lat":
            outs.pop(0)[...] = acc
            continue
        if mode == "silu":
            outs.pop(0)[...] = (acc * jax.nn.sigmoid(acc)).astype(BF16)
            continue
        pattern_outs = [outs.pop(0) for _ in DILATED_PATTERNS]
        for h in range(B_HEADS):
            xh = acc[:, h * B_HDIM:(h + 1) * B_HDIM]
            if mode == "rope":
                xh = (xh * cos + pltpu.roll(xh, B_HDIM // 2, 1) * sin) * scale
            seg_ref[n_dil, h] = xh
            for o_ref, (_, dil) in zip(pattern_outs, DILATED_PATTERNS):
                if dil == 1:
                    o_ref[:, h * B_HDIM:(h + 1) * B_HDIM] = xh.astype(BF16)
                    continue
                for r in range(dil):
                    col = r * SEG + h * B_HDIM
                    o_ref[:, col:col + B_HDIM] = seg_ref[n_dil, h, pl.ds(r, tm // dil, stride=dil), :].astype(BF16)
        n_dil += 1


def _in_proj(x2, w_lat, w_rest, cos_b, sin_b, s, tm):
    n = x2.shape[0]
    n_pos = s // tm
    row = lambda i: (i, 0)
    out_specs, out_shape = [], []
    for mode in SEG_MODES:
        if mode in ("lat", "silu"):
            out_specs.append(pl.BlockSpec((tm, SEG), row))
            out_shape.append(jax.ShapeDtypeStruct((n, SEG), F32 if mode == "lat" else BF16))
        else:
            out_specs += [pl.BlockSpec((tm // d, d * SEG), row) for _, d in DILATED_PATTERNS]
            out_shape += [jax.ShapeDtypeStruct((n // d, d * SEG), BF16) for _, d in DILATED_PATTERNS]
    flat = pl.pallas_call(
        _in_proj_kernel,
        grid=(n // tm,),
        in_specs=[
            pl.BlockSpec((tm, D_MODEL), row),
            pl.BlockSpec((D_MODEL, SEG), lambda i: (0, 0), pipeline_mode=pl.Buffered(1)),
            pl.BlockSpec((D_MODEL, (N_SEG - 1) * SEG), lambda i: (0, 0), pipeline_mode=pl.Buffered(1)),
            pl.BlockSpec((tm, B_HDIM), lambda i: (i % n_pos, 0)),
            pl.BlockSpec((tm, B_HDIM), lambda i: (i % n_pos, 0)),
        ],
        out_specs=out_specs,
        out_shape=out_shape,
        scratch_shapes=[pltpu.VMEM((N_DIL_OPERANDS, B_HEADS, tm, B_HDIM), F32)],
        compiler_params=_cparams(("parallel",)),
        name="in_proj",
    )(x2, w_lat, w_rest, cos_b, sin_b)
    flat = list(flat)
    result = []
    for mode in SEG_MODES:
        if mode in ("lat", "silu"):
            result.append(flat.pop(0))
        else:
            result.append(tuple(flat.pop(0) for _ in DILATED_PATTERNS))
    return result


def _rms(x, g):
    return x * lax.rsqrt(jnp.mean(x * x, axis=-1, keepdims=True) + RMS_EPS) * g


def _mla_up_kernel(lat_ref, gq_ref, gkv_ref, wuqt_ref, wuk_ref, wuvt_ref,
                   cosk_ref, sink_ref, cost_ref, sint_ref, qt_ref, k_ref, vt_ref):
    lat = lat_ref[0]
    cqn = _rms(lat[:, :Q_LORA], gq_ref[...]).astype(BF16)
    ckvn = _rms(lat[:, Q_LORA:Q_LORA + KV_LORA], gkv_ref[...]).astype(BF16)
    kr = lat[:, Q_LORA + KV_LORA:Q_LORA + KV_LORA + LANES]
    lane = lax.broadcasted_iota(jnp.int32, kr.shape, 1)
    half = A_ROPE // 2
    rot = jnp.where(lane < half, pltpu.roll(kr, LANES - half, 1), pltpu.roll(kr, half, 1))
    kpe = (kr * cosk_ref[...] + rot * sink_ref[...]).astype(BF16)
    ct = cost_ref[...]
    st = sint_ref[...]
    for h in range(A_HEADS):
        qt = _nt_dot(wuqt_ref[h], cqn)
        x1 = qt[A_NOPE:A_NOPE + half]
        x2 = qt[A_NOPE + half:]
        qt_ref[0, h, 0:A_NOPE, :] = (qt[:A_NOPE] * A_SCALE).astype(BF16)
        qt_ref[0, h, A_NOPE:A_NOPE + half, :] = ((x1 * ct - x2 * st) * A_SCALE).astype(BF16)
        qt_ref[0, h, A_NOPE + half:, :] = ((x2 * ct + x1 * st) * A_SCALE).astype(BF16)
        k_ref[0, h, :, 0:A_NOPE] = jnp.dot(ckvn, wuk_ref[h], preferred_element_type=F32).astype(BF16)
        k_ref[0, h, :, A_NOPE:] = kpe[:, :A_ROPE]
        vt_ref[0, h, 0:A_VDIM, :] = _nt_dot(wuvt_ref[h], ckvn).astype(BF16)
        vt_ref[0, h, A_VDIM:, :] = jnp.ones((A_VROWS - A_VDIM, vt_ref.shape[3]), BF16)


def _mla_up(lat3, gq, gkv, wuqt, wuk, wuvt, cosk, sink, cost, sint, tm):
    b, s, _ = lat3.shape
    full = lambda *shape: pl.BlockSpec(shape, lambda bi, si: (0,) * len(shape))
    return pl.pallas_call(
        _mla_up_kernel,
        grid=(b, s // tm),
        in_specs=[
            pl.BlockSpec((1, tm, SEG), lambda bi, si: (bi, si, 0)),
            full(1, Q_LORA), full(1, KV_LORA),
            full(A_HEADS, A_QK, Q_LORA), full(A_HEADS, KV_LORA, A_NOPE), full(A_HEADS, A_VDIM, KV_LORA),
            pl.BlockSpec((tm, LANES), lambda bi, si: (si, 0)),
            pl.BlockSpec((tm, LANES), lambda bi, si: (si, 0)),
            pl.BlockSpec((A_ROPE // 2, tm), lambda bi, si: (0, si)),
            pl.BlockSpec((A_ROPE // 2, tm), lambda bi, si: (0, si)),
        ],
        out_specs=[
            pl.BlockSpec((1, A_HEADS, A_QK, tm), lambda bi, si: (bi, 0, 0, si)),
            pl.BlockSpec((1, A_HEADS, tm, A_QK), lambda bi, si: (bi, 0, si, 0)),
            pl.BlockSpec((1, A_HEADS, A_VROWS, tm), lambda bi, si: (bi, 0, 0, si)),
        ],
        out_shape=[
            jax.ShapeDtypeStruct((b, A_HEADS, A_QK, s), BF16),
            jax.ShapeDtypeStruct((b, A_HEADS, s, A_QK), BF16),
            jax.ShapeDtypeStruct((b, A_HEADS, A_VROWS, s), BF16),
        ],
        compiler_params=_cparams(("parallel", "parallel")),
        name="mla_up",
    )(lat3, gq, gkv, wuqt, wuk, wuvt, cosk, sink, cost, sint)


FLASH_BUFS = 2
FLASH_UNROLL = 4
FLASH_BQ_MAX = 4096
FLASH_HEADROOM = 64.0


def _mla_flash_kernel(qt_ref, k_ref, vt_ref, o_ref, s_ref, p_ref, acc_ref, *, bk, nk):
    bq = qt_ref.shape[3]

    def key_block(t):
        return k_ref[0, 0, pl.ds(pl.multiple_of(t * bk, bk), bk), :]

    def value_block(t):
        return vt_ref[0, 0, :, pl.ds(pl.multiple_of(t * bk, bk), bk)]

    def finish():
        acc = acc_ref[...]
        o_ref[0] = (acc[:A_VDIM] * (1.0 / acc[A_VDIM:A_VDIM + 1])).T.astype(o_ref.dtype)

    def scores_exp(t, slot, carry):
        m_cur, m_prev, _, excess = carry
        st = jnp.dot(key_block(t), qt_ref[0, 0], preferred_element_type=F32)
        cmax = jnp.max(st, axis=0, keepdims=True)
        p_ref[slot] = jnp.exp2((st - m_cur).astype(BF16))
        return jnp.maximum(m_cur, cmax), m_cur, jnp.exp2(m_prev - m_cur), jnp.maximum(excess, cmax - m_cur)

    def values(t, slot, alpha):
        acc_ref[...] = alpha * acc_ref[...] + jnp.dot(value_block(t), p_ref[slot], preferred_element_type=F32)

    def step(t, t_slot, carry):
        alpha_t = carry[2]
        carry = scores_exp(t + 1, (t_slot + 1) % FLASH_BUFS, carry)
        values(t, t_slot, alpha_t)
        return carry

    s0 = jnp.dot(key_block(0), qt_ref[0, 0], preferred_element_type=F32)
    s_ref[...] = s0
    m0 = jnp.max(s0, axis=0, keepdims=True)
    p_ref[0] = jnp.exp2((s_ref[...] - m0).astype(BF16))
    acc_ref[...] = jnp.zeros(acc_ref.shape, F32)
    carry = (m0, m0, jnp.ones((1, bq), F32), jnp.zeros((1, bq), F32))

    def group(j, carry):
        for c in range(FLASH_UNROLL):
            carry = step(FLASH_UNROLL * j + c, c % FLASH_BUFS, carry)
        return carry

    n_groups = (nk - 1) // FLASH_UNROLL
    carry = lax.fori_loop(0, n_groups, group, carry)
    for t in range(FLASH_UNROLL * n_groups, nk - 1):
        carry = step(t, t % FLASH_BUFS, carry)
    values(nk - 1, (nk - 1) % FLASH_BUFS, carry[2])
    finish()

    @pl.when(jnp.max(carry[3]) > FLASH_HEADROOM)
    def _():
        acc_ref[...] = jnp.zeros(acc_ref.shape, F32)

        def body(t, m):
            st = jnp.dot(key_block(t), qt_ref[0, 0], preferred_element_type=F32)
            m_new = jnp.maximum(m, jnp.max(st, axis=0, keepdims=True))
            p = jnp.exp2((st - m_new).astype(BF16))
            acc_ref[...] = jnp.exp2(m - m_new) * acc_ref[...] + jnp.dot(value_block(t), p, preferred_element_type=F32)
            return m_new

        lax.fori_loop(0, nk, body, jnp.full((1, bq), NEG_BIG, F32))
        finish()


def _mla_flash(qt, k, vt, bq, bk):
    b, h, _, s = qt.shape
    nk = s // bk
    assert nk >= 2
    kern = functools.partial(_mla_flash_kernel, bk=bk, nk=nk)
    return pl.pallas_call(
        kern,
        grid=(b, h, s // bq),
        in_specs=[
            pl.BlockSpec((1, 1, A_QK, bq), lambda bi, hi, qi: (bi, hi, 0, qi)),
            pl.BlockSpec((1, 1, s, A_QK), lambda bi, hi, qi: (bi, hi, 0, 0)),
            pl.BlockSpec((1, 1, A_VROWS, s), lambda bi, hi, qi: (bi, hi, 0, 0)),
        ],
        out_specs=pl.BlockSpec((1, bq, A_VDIM), lambda bi, hi, qi: (bi, qi, hi)),
        out_shape=jax.ShapeDtypeStruct((b, s, A_WIDTH), MIX_DTYPE),
        scratch_shapes=[pltpu.VMEM((bk, bq), F32), pltpu.VMEM((FLASH_BUFS, bk, bq), BF16),
                        pltpu.VMEM((A_VROWS, bq), F32)],
        compiler_params=_cparams(("parallel", "parallel", "arbitrary")),
        name="mla_flash",
    )(qt, k, vt)


def _dilated_kernel(q_ref, kp_ref, kc_ref, kn_ref, vp_ref, vc_ref, vn_ref, o_ref, lse_ref, *, bq, nblk):
    i = pl.program_id(2)
    nkeys = DIL_SUBQ + 2 * HALF_WIN
    n_sub = bq // DIL_SUBQ
    a = lax.broadcasted_iota(jnp.int32, (DIL_SUBQ, nkeys), 0)
    c = lax.broadcasted_iota(jnp.int32, (DIL_SUBQ, nkeys), 1)
    band = (c >= a) & (c <= a + 2 * HALF_WIN)
    lane = lax.broadcasted_iota(jnp.int32, (DIL_SUBQ, LANES), 1)

    def window(p_ref, c_ref, n_ref, j, hs):
        lo = j * DIL_SUBQ - HALF_WIN
        hi = lo + nkeys
        parts = []
        if lo < 0:
            parts.append(p_ref[0, :, hs])
        parts.append(c_ref[0, max(lo, 0):min(hi, bq), hs])
        if hi > bq:
            parts.append(n_ref[0, :, hs])
        return jnp.concatenate(parts, axis=0) if len(parts) > 1 else parts[0]

    for j in range(n_sub):
        mask = band
        if j == 0:
            mask = mask & ((c >= HALF_WIN) | (i > 0))
        if j == n_sub - 1:
            mask = mask & ((c < DIL_SUBQ + HALF_WIN) | (i < nblk - 1))
        bias = jnp.where(mask, 0.0, NEG_BIG)
        rows = slice(j * DIL_SUBQ, (j + 1) * DIL_SUBQ)
        lse_all = jnp.zeros((DIL_SUBQ, LANES), F32)
        for h in range(B_HEADS):
            hs = slice(h * B_HDIM, (h + 1) * B_HDIM)
            k = window(kp_ref, kc_ref, kn_ref, j, hs)
            v = window(vp_ref, vc_ref, vn_ref, j, hs)
            sc = _nt_dot(q_ref[0, rows, hs], k) + bias
            m = jnp.max(sc, axis=1, keepdims=True)
            p = jnp.exp2(sc - m).astype(BF16)
            both = jnp.dot(p, jnp.concatenate([v, jnp.ones_like(v)], axis=1), preferred_element_type=F32)
            den = both[:, B_HDIM:]
            o_ref[0, rows, hs] = (both[:, :B_HDIM] * (1.0 / den)).astype(o_ref.dtype)
            lse_all = jnp.where(lane == h, m + jnp.log2(den), lse_all)
        lse_ref[0, rows, :] = lse_all


def _dilated(qv, kv, vv, b, dil, bq):
    m_len = qv.shape[0] // b
    nblk = m_len // bq
    sub = bq // HALF_WIN
    n_half = m_len // HALF_WIN
    view = lambda t: t.reshape(b, m_len, dil * B_WIDTH)
    cur = pl.BlockSpec((1, bq, B_WIDTH), lambda bi, r, i: (bi, i, r))
    prev = pl.BlockSpec((1, HALF_WIN, B_WIDTH), lambda bi, r, i: (bi, jnp.maximum(i * sub - 1, 0), r))
    nxt = pl.BlockSpec((1, HALF_WIN, B_WIDTH), lambda bi, r, i: (bi, jnp.minimum((i + 1) * sub, n_half - 1), r))
    kern = functools.partial(_dilated_kernel, bq=bq, nblk=nblk)
    o, lse = pl.pallas_call(
        kern,
        grid=(b, dil, nblk),
        in_specs=[cur, prev, cur, nxt, prev, cur, nxt],
        out_specs=[cur, pl.BlockSpec((1, bq, LANES), lambda bi, r, i: (bi, i, r))],
        out_shape=[jax.ShapeDtypeStruct((b, m_len, dil * B_WIDTH), MIX_DTYPE),
                   jax.ShapeDtypeStruct((b, m_len, dil * LANES), F32)],
        compiler_params=_cparams(("parallel", "parallel", "parallel")),
        name=f"dilated_d{dil}",
    )(view(qv), view(kv), view(kv), view(kv), view(vv), view(vv), view(vv))
    return o.reshape(b * m_len, dil * B_WIDTH), lse.reshape(b * m_len, dil * LANES)


OUT_GROUPS = 4


def _out_proj_kernel(oa_ref, o1_ref, o2_ref, o3_ref, l1_ref, l2_ref, l3_ref, ga_ref, gb_ref, x_ref,
                     w_ref, goa_ref, gob_ref, lng_ref, lnb_ref, y_ref, on_ref, ln_ref):
    tm = oa_ref.shape[0]
    for i, (o_ref, l_ref, (_, dil)) in enumerate(zip((o1_ref, o2_ref, o3_ref), (l1_ref, l2_ref, l3_ref),
                                                     DILATED_PATTERNS)):
        for r in range(dil):
            rows = pl.ds(r, tm // dil, stride=dil) if dil > 1 else slice(None)
            ln_ref[i, rows, :] = l_ref[:, r * LANES:(r + 1) * LANES]
            for h in range(B_HEADS):
                col = r * B_WIDTH + h * B_HDIM
                on_ref[i, h, rows, :] = o_ref[:, col:col + B_HDIM].astype(F32)
    for g in range(OUT_GROUPS):
        rows = slice(g * tm // OUT_GROUPS, (g + 1) * tm // OUT_GROUPS)
        za = (_rms(oa_ref[rows, :].astype(F32), goa_ref[...]) * ga_ref[rows, :].astype(F32)).astype(BF16)
        l1, l2, l3 = ln_ref[0, rows, :], ln_ref[1, rows, :], ln_ref[2, rows, :]
        mx = jnp.maximum(jnp.maximum(l1, l2), l3)
        e1, e2, e3 = jnp.exp2(l1 - mx), jnp.exp2(l2 - mx), jnp.exp2(l3 - mx)
        inv = 1.0 / (e1 + e2 + e3)
        w1, w2, w3 = e1 * inv, e2 * inv, e3 * inv
        parts = []
        for h in range(B_HEADS):
            parts.append(w1[:, h:h + 1] * on_ref[0, h, rows, :] + w2[:, h:h + 1] * on_ref[1, h, rows, :]
                         + w3[:, h:h + 1] * on_ref[2, h, rows, :])
        ob = jnp.concatenate(parts, axis=1)
        zb = (_rms(ob, gob_ref[...]) * gb_ref[rows, :].astype(F32)).astype(BF16)
        out = (jnp.dot(za, w_ref[:A_WIDTH, :], preferred_element_type=F32)
               + jnp.dot(zb, w_ref[A_WIDTH:, :], preferred_element_type=F32))
        r = DEEPNORM_ALPHA * x_ref[rows, :] + out
        mu = jnp.mean(r, axis=-1, keepdims=True)
        d = r - mu
        var = jnp.mean(d * d, axis=-1, keepdims=True)
        y_ref[rows, :] = d * lax.rsqrt(var + LN_EPS) * lng_ref[...] + lnb_ref[...]


def _out_proj(oa, obs, lses, ga, gb, x2, w_out, goa, gob, lng, lnb, tm):
    n = x2.shape[0]
    row = lambda w: pl.BlockSpec((tm, w), lambda i: (i, 0))
    view = lambda w: [pl.BlockSpec((tm // d, d * w), lambda i: (i, 0)) for _, d in DILATED_PATTERNS]
    full = lambda *shape: pl.BlockSpec(shape, lambda i: (0,) * len(shape))
    n_pat = len(DILATED_PATTERNS)
    return pl.pallas_call(
        _out_proj_kernel,
        grid=(n // tm,),
        in_specs=[row(A_WIDTH)] + view(B_WIDTH) + view(LANES) + [row(A_WIDTH), row(B_WIDTH), row(D_MODEL),
                  pl.BlockSpec((A_WIDTH + B_WIDTH, D_MODEL), lambda i: (0, 0), pipeline_mode=pl.Buffered(1)),
                  full(1, A_WIDTH), full(1, B_WIDTH),
                  full(1, D_MODEL), full(1, D_MODEL)],
        out_specs=row(D_MODEL),
        out_shape=jax.ShapeDtypeStruct((n, D_MODEL), F32),
        scratch_shapes=[pltpu.VMEM((n_pat, B_HEADS, tm, B_HDIM), F32), pltpu.VMEM((n_pat, tm, LANES), F32)],
        compiler_params=_cparams(("parallel",)),
        name="out_proj",
    )(oa, *obs, *lses, ga, gb, x2, w_out, goa, gob, lng, lnb)


def _rope_tables(s):
    pos = jnp.arange(s, dtype=F32)

    def cos_sin(d):
        half = d // 2
        inv = ROPE_THETA ** (-jnp.arange(half, dtype=F32) * 2.0 / d)
        ang = pos[:, None] * inv[None, :]
        return jnp.cos(ang), jnp.sin(ang)

    cb, sb = cos_sin(B_HDIM)
    cos_b = jnp.concatenate([cb, cb], axis=1)
    sin_b = jnp.concatenate([-sb, sb], axis=1)
    ca, sa = cos_sin(A_ROPE)
    zeros = jnp.zeros((s, LANES - A_ROPE), F32)
    cos_k = jnp.concatenate([ca, ca, zeros], axis=1)
    sin_k = jnp.concatenate([-sa, sa, zeros], axis=1)
    return cos_b, sin_b, cos_k, sin_k, ca.T, sa.T


def _prep_weights(w_in, g_qn, w_uq, g_kvn, w_ukv, g_oa, g_ob, w_out, ln_g, ln_b):
    n_lat = Q_LORA + KV_LORA + A_ROPE
    w_lat = jnp.pad(w_in[:, :n_lat], ((0, 0), (0, LAT_PAD))).astype(BF16)
    w_rest = w_in[:, n_lat:].astype(BF16)
    wuqt = w_uq.reshape(Q_LORA, A_HEADS, A_QK).transpose(1, 2, 0).astype(BF16)
    wukv = w_ukv.reshape(KV_LORA, A_HEADS, A_NOPE + A_VDIM)
    wuk = wukv[:, :, :A_NOPE].transpose(1, 0, 2).astype(BF16)
    wuvt = wukv[:, :, A_NOPE:].transpose(1, 2, 0).astype(BF16)
    return dict(w_lat=w_lat, w_rest=w_rest, gq=g_qn[None, :], gkv=g_kvn[None, :], wuqt=wuqt, wuk=wuk, wuvt=wuvt,
                goa=g_oa[None, :], gob=g_ob[None, :], w_out=w_out.astype(BF16),
                lng=ln_g[None, :], lnb=ln_b[None, :])


def _tiles(s):
    pick = lambda pref, n: pref if n % pref == 0 else n
    bq = pick(min(FLASH_BQ_MAX, s), s)
    return dict(tm_in=pick(256, s), tm_up=pick(1024, s), bq=bq, bk=pick(512, s), tm_out=pick(512, s))


def _encoder_layer(x, p):
    b, s, _ = x.shape
    longest = max(w // 2 for w, _ in DILATED_PATTERNS)
    assert s % longest == 0, "sequence length must be a multiple of the largest dilated chunk"
    t = _tiles(s)
    cos_b, sin_b, cos_k, sin_k, cos_t, sin_t = _rope_tables(s)
    x2 = x.reshape(b * s, D_MODEL)
    lat, qvs, kvs, vvs, ga, gb = _in_proj(x2, p["w_lat"], p["w_rest"], cos_b, sin_b, s, t["tm_in"])
    qt, k, vt = _mla_up(lat.reshape(b, s, SEG), p["gq"], p["gkv"], p["wuqt"], p["wuk"], p["wuvt"],
                        cos_k, sin_k, cos_t, sin_t, t["tm_up"])
    oa = _mla_flash(qt, k, vt, t["bq"], t["bk"]).reshape(b * s, A_WIDTH)
    obs, lses = [], []
    for i, (window, dil) in enumerate(DILATED_PATTERNS):
        assert window // (2 * dil) == HALF_WIN
        o, lse = _dilated(qvs[i], kvs[i], vvs[i], b, dil, min(DIL_BQ, s // dil))
        obs.append(o)
        lses.append(lse)
    y = _out_proj(oa, obs, lses, ga, gb, x2, p["w_out"], p["goa"], p["gob"], p["lng"], p["lnb"], t["tm_out"])
    return y.reshape(b, s, D_MODEL)


def kernel(x_prompt, x_sample, w_in, g_qn, w_uq, g_kvn, w_ukv, g_oa, g_ob, w_out, ln_g, ln_b):
    layers = [_prep_weights(w_in[l], g_qn[l], w_uq[l], g_kvn[l], w_ukv[l], g_oa[l], g_ob[l],
                            w_out[l], ln_g[l], ln_b[l]) for l in range(DEPTH)]

    def trunk(x):
        for p in layers:
            x = _encoder_layer(x, p)
        return x

    return (trunk(x_prompt), trunk(x_sample))
```

```python
import functools

import jax
import jax.numpy as jnp
from jax import lax
from jax.experimental import pallas as pl
from jax.experimental.pallas import tpu as pltpu

F32 = jnp.float32
BF16 = jnp.bfloat16

D_MODEL = 2048
DEPTH = 1
A_HEADS = 8
A_NOPE = 128
A_ROPE = 64
A_VDIM = 128
A_QK = A_NOPE + A_ROPE
A_WIDTH = A_HEADS * A_VDIM
BF16_SUBLANES = 16
A_VROWS = A_VDIM + BF16_SUBLANES
Q_LORA = 512
KV_LORA = 256
B_HEADS = 8
B_HDIM = 128
B_WIDTH = B_HEADS * B_HDIM
DILATED_PATTERNS = ((128, 1), (512, 4), (2048, 16))
ROPE_THETA = 10000.0
RMS_EPS = 1e-6
LN_EPS = 1e-5
NEG_BIG = -1e30
DEEPNORM_ALPHA = (2 * DEPTH) ** 0.25
LOG2E = 1.4426950408889634
A_SCALE = A_QK ** -0.5 * LOG2E
B_SCALE = B_HDIM ** -0.5 * LOG2E

LANES = 128
SEG = 1024
LAT_PAD = SEG - (Q_LORA + KV_LORA + A_ROPE)
HALF_WIN = 64
DIL_SUBQ = 128
DIL_BQ = 1024
VMEM_LIMIT = 56 * 1024 * 1024
MIX_DTYPE = BF16


def _cparams(semantics):
    return pltpu.CompilerParams(dimension_semantics=semantics, vmem_limit_bytes=VMEM_LIMIT)


def _nt_dot(a, b):
    return lax.dot_general(a, b, (((1,), (1,)), ((), ())), preferred_element_type=F32)


SEG_MODES = ("lat", "rope", "rope", "plain", "silu", "silu")
N_SEG = len(SEG_MODES)
SEG_SCALES = (1.0, B_SCALE, 1.0, 1.0, 1.0, 1.0)
N_DIL_OPERANDS = sum(m in ("rope", "plain") for m in SEG_MODES)


def _in_proj_kernel(x_ref, wl_ref, w_ref, cos_ref, sin_ref, *refs):
    outs, seg_ref = list(refs[:-1]), refs[-1]
    tm = x_ref.shape[0]
    xb = x_ref[...].astype(BF16)
    cos = cos_ref[...]
    sin = sin_ref[...]
    n_dil = 0
    for j, (mode, scale) in enumerate(zip(SEG_MODES, SEG_SCALES)):
        w = wl_ref[...] if j == 0 else w_ref[:, (j - 1) * SEG:j * SEG]
        acc = jnp.dot(xb, w, preferred_element_type=F32)
        if mode == "lat":
            outs.pop(0)[...] = acc
            continue
        if mode == "silu":
            outs.pop(0)[...] = (acc * jax.nn.sigmoid(acc)).astype(BF16)
            continue
        pattern_outs = [outs.pop(0) for _ in DILATED_PATTERNS]
        for h in range(B_HEADS):
            xh = acc[:, h * B_HDIM:(h + 1) * B_HDIM]
            if mode == "rope":
                xh = (xh * cos + pltpu.roll(xh, B_HDIM // 2, 1) * sin) * scale
            seg_ref[n_dil, h] = xh
            for o_ref, (_, dil) in zip(pattern_outs, DILATED_PATTERNS):
                if dil == 1:
                    o_ref[:, h * B_HDIM:(h + 1) * B_HDIM] = xh.astype(BF16)
                    continue
                for r in range(dil):
                    col = r * SEG + h * B_HDIM
                    o_ref[:, col:col + B_HDIM] = seg_ref[n_dil, h, pl.ds(r, tm // dil, stride=dil), :].astype(BF16)
        n_dil += 1


def _in_proj(x2, w_lat, w_rest, cos_b, sin_b, s, tm):
    n = x2.shape[0]
    n_pos = s // tm
    row = lambda i: (i, 0)
    out_specs, out_shape = [], []
    for mode in SEG_MODES:
        if mode in ("lat", "silu"):
            out_specs.append(pl.BlockSpec((tm, SEG), row))
            out_shape.append(jax.ShapeDtypeStruct((n, SEG), F32 if mode == "lat" else BF16))
        else:
            out_specs += [pl.BlockSpec((tm // d, d * SEG), row) for _, d in DILATED_PATTERNS]
            out_shape += [jax.ShapeDtypeStruct((n // d, d * SEG), BF16) for _, d in DILATED_PATTERNS]
    flat = pl.pallas_call(
        _in_proj_kernel,
        grid=(n // tm,),
        in_specs=[
            pl.BlockSpec((tm, D_MODEL), row),
            pl.BlockSpec((D_MODEL, SEG), lambda i: (0, 0), pipeline_mode=pl.Buffered(1)),
            pl.BlockSpec((D_MODEL, (N_SEG - 1) * SEG), lambda i: (0, 0), pipeline_mode=pl.Buffered(1)),
            pl.BlockSpec((tm, B_HDIM), lambda i: (i % n_pos, 0)),
            pl.BlockSpec((tm, B_HDIM), lambda i: (i % n_pos, 0)),
        ],
        out_specs=out_specs,
        out_shape=out_shape,
        scratch_shapes=[pltpu.VMEM((N_DIL_OPERANDS, B_HEADS, tm, B_HDIM), F32)],
        compiler_params=_cparams(("parallel",)),
        name="in_proj",
    )(x2, w_lat, w_rest, cos_b, sin_b)
    flat = list(flat)
    result = []
    for mode in SEG_MODES:
        if mode in ("lat", "silu"):
            result.append(flat.pop(0))
        else:
            result.append(tuple(flat.pop(0) for _ in DILATED_PATTERNS))
    return result


def _rms(x, g):
    return x * lax.rsqrt(jnp.mean(x * x, axis=-1, keepdims=True) + RMS_EPS) * g


def _mla_up_kernel(lat_ref, gq_ref, gkv_ref, wuqt_ref, wuk_ref, wuvt_ref,
                   cosk_ref, sink_ref, cost_ref, sint_ref, qt_ref, k_ref, vt_ref):
    lat = lat_ref[0]
    cqn = _rms(lat[:, :Q_LORA], gq_ref[...]).astype(BF16)
    ckvn = _rms(lat[:, Q_LORA:Q_LORA + KV_LORA], gkv_ref[...]).astype(BF16)
    kr = lat[:, Q_LORA + KV_LORA:Q_LORA + KV_LORA + LANES]
    lane = lax.broadcasted_iota(jnp.int32, kr.shape, 1)
    half = A_ROPE // 2
    rot = jnp.where(lane < half, pltpu.roll(kr, LANES - half, 1), pltpu.roll(kr, half, 1))
    kpe = (kr * cosk_ref[...] + rot * sink_ref[...]).astype(BF16)
    ct = cost_ref[...]
    st = sint_ref[...]
    for h in range(A_HEADS):
        qt = _nt_dot(wuqt_ref[h], cqn)
        x1 = qt[A_NOPE:A_NOPE + half]
        x2 = qt[A_NOPE + half:]
        qt_ref[0, h, 0:A_NOPE, :] = (qt[:A_NOPE] * A_SCALE).astype(BF16)
        qt_ref[0, h, A_NOPE:A_NOPE + half, :] = ((x1 * ct - x2 * st) * A_SCALE).astype(BF16)
        qt_ref[0, h, A_NOPE + half:, :] = ((x2 * ct + x1 * st) * A_SCALE).astype(BF16)
        k_ref[0, h, :, 0:A_NOPE] = jnp.dot(ckvn, wuk_ref[h], preferred_element_type=F32).astype(BF16)
        k_ref[0, h, :, A_NOPE:] = kpe[:, :A_ROPE]
        vt_ref[0, h, 0:A_VDIM, :] = _nt_dot(wuvt_ref[h], ckvn).astype(BF16)
        vt_ref[0, h, A_VDIM:, :] = jnp.ones((A_VROWS - A_VDIM, vt_ref.shape[3]), BF16)


def _mla_up(lat3, gq, gkv, wuqt, wuk, wuvt, cosk, sink, cost, sint, tm):
    b, s, _ = lat3.shape
    full = lambda *shape: pl.BlockSpec(shape, lambda bi, si: (0,) * len(shape))
    return pl.pallas_call(
        _mla_up_kernel,
        grid=(b, s // tm),
        in_specs=[
            pl.BlockSpec((1, tm, SEG), lambda bi, si: (bi, si, 0)),
            full(1, Q_LORA), full(1, KV_LORA),
            full(A_HEADS, A_QK, Q_LORA), full(A_HEADS, KV_LORA, A_NOPE), full(A_HEADS, A_VDIM, KV_LORA),
            pl.BlockSpec((tm, LANES), lambda bi, si: (si, 0)),
            pl.BlockSpec((tm, LANES), lambda bi, si: (si, 0)),
            pl.BlockSpec((A_ROPE // 2, tm), lambda bi, si: (0, si)),
            pl.BlockSpec((A_ROPE // 2, tm), lambda bi, si: (0, si)),
        ],
        out_specs=[
            pl.BlockSpec((1, A_HEADS, A_QK, tm), lambda bi, si: (bi, 0, 0, si)),
            pl.BlockSpec((1, A_HEADS, tm, A_QK), lambda bi, si: (bi, 0, si, 0)),
            pl.BlockSpec((1, A_HEADS, A_VROWS, tm), lambda bi, si: (bi, 0, 0, si)),
        ],
        out_shape=[
            jax.ShapeDtypeStruct((b, A_HEADS, A_QK, s), BF16),
            jax.ShapeDtypeStruct((b, A_HEADS, s, A_QK), BF16),
            jax.ShapeDtypeStruct((b, A_HEADS, A_VROWS, s), BF16),
        ],
        compiler_params=_cparams(("parallel", "parallel")),
        name="mla_up",
    )(lat3, gq, gkv, wuqt, wuk, wuvt, cosk, sink, cost, sint)


FLASH_BUFS = 2
FLASH_UNROLL = 4
FLASH_BQ_MAX = 4096
FLASH_HEADROOM = 64.0


def _mla_flash_kernel(qt_ref, k_ref, vt_ref, o_ref, s_ref, p_ref, acc_ref, *, bk, nk):
    bq = qt_ref.shape[3]

    def key_block(t):
        return k_ref[0, 0, pl.ds(pl.multiple_of(t * bk, bk), bk), :]

    def value_block(t):
        return vt_ref[0, 0, :, pl.ds(pl.multiple_of(t * bk, bk), bk)]

    def finish():
        acc = acc_ref[...]
        o_ref[0] = (acc[:A_VDIM] * (1.0 / acc[A_VDIM:A_VDIM + 1])).T.astype(o_ref.dtype)

    def scores_exp(t, slot, carry):
        m_cur, m_prev, _, excess = carry
        st = jnp.dot(key_block(t), qt_ref[0, 0], preferred_element_type=F32)
        cmax = jnp.max(st, axis=0, keepdims=True)
        p_ref[slot] = jnp.exp2((st - m_cur).astype(BF16))
        return jnp.maximum(m_cur, cmax), m_cur, jnp.exp2(m_prev - m_cur), jnp.maximum(excess, cmax - m_cur)

    def values(t, slot, alpha):
        acc_ref[...] = alpha * acc_ref[...] + jnp.dot(value_block(t), p_ref[slot], preferred_element_type=F32)

    def step(t, t_slot, carry):
        alpha_t = carry[2]
        carry = scores_exp(t + 1, (t_slot + 1) % FLASH_BUFS, carry)
        values(t, t_slot, alpha_t)
        return carry

    s0 = jnp.dot(key_block(0), qt_ref[0, 0], preferred_element_type=F32)
    s_ref[...] = s0
    m0 = jnp.max(s0, axis=0, keepdims=True)
    p_ref[0] = jnp.exp2((s_ref[...] - m0).astype(BF16))
    acc_ref[...] = jnp.zeros(acc_ref.shape, F32)
    carry = (m0, m0, jnp.ones((1, bq), F32), jnp.zeros((1, bq), F32))

    def group(j, carry):
        for c in range(FLASH_UNROLL):
            carry = step(FLASH_UNROLL * j + c, c % FLASH_BUFS, carry)
        return carry

    n_groups = (nk - 1) // FLASH_UNROLL
    carry = lax.fori_loop(0, n_groups, group, carry)
    for t in range(FLASH_UNROLL * n_groups, nk - 1):
        carry = step(t, t % FLASH_BUFS, carry)
    values(nk - 1, (nk - 1) % FLASH_BUFS, carry[2])
    finish()

    @pl.when(jnp.max(carry[3]) > FLASH_HEADROOM)
    def _():
        acc_ref[...] = jnp.zeros(acc_ref.shape, F32)

        def body(t, m):
            st = jnp.dot(key_block(t), qt_ref[0, 0], preferred_element_type=F32)
            m_new = jnp.maximum(m, jnp.max(st, axis=0, keepdims=True))
            p = jnp.exp2((st - m_new).astype(BF16))
            acc_ref[...] = jnp.exp2(m - m_new) * acc_ref[...] + jnp.dot(value_block(t), p, preferred_element_type=F32)
            return m_new

        lax.fori_loop(0, nk, body, jnp.full((1, bq), NEG_BIG, F32))
        finish()


def _mla_flash(qt, k, vt, bq, bk):
    b, h, _, s = qt.shape
    nk = s // bk
    assert nk >= 2
    kern = functools.partial(_mla_flash_kernel, bk=bk, nk=nk)
    return pl.pallas_call(
        kern,
        grid=(b, h, s // bq),
        in_specs=[
            pl.BlockSpec((1, 1, A_QK, bq), lambda bi, hi, qi: (bi, hi, 0, qi)),
            pl.BlockSpec((1, 1, s, A_QK), lambda bi, hi, qi: (bi, hi, 0, 0)),
            pl.BlockSpec((1, 1, A_VROWS, s), lambda bi, hi, qi: (bi, hi, 0, 0)),
        ],
        out_specs=pl.BlockSpec((1, bq, A_VDIM), lambda bi, hi, qi: (bi, qi, hi)),
        out_shape=jax.ShapeDtypeStruct((b, s, A_WIDTH), MIX_DTYPE),
        scratch_shapes=[pltpu.VMEM((bk, bq), F32), pltpu.VMEM((FLASH_BUFS, bk, bq), BF16),
                        pltpu.VMEM((A_VROWS, bq), F32)],
        compiler_params=_cparams(("parallel", "parallel", "arbitrary")),
        name="mla_flash",
    )(qt, k, vt)


def _dilated_kernel(q_ref, kp_ref, kc_ref, kn_ref, vp_ref, vc_ref, vn_ref, o_ref, lse_ref, *, bq, nblk):
    i = pl.program_id(2)
    nkeys = DIL_SUBQ + 2 * HALF_WIN
    n_sub = bq // DIL_SUBQ
    a = lax.broadcasted_iota(jnp.int32, (DIL_SUBQ, nkeys), 0)
    c = lax.broadcasted_iota(jnp.int32, (DIL_SUBQ, nkeys), 1)
    band = (c >= a) & (c <= a + 2 * HALF_WIN)
    lane = lax.broadcasted_iota(jnp.int32, (DIL_SUBQ, LANES), 1)

    def window(p_ref, c_ref, n_ref, j, hs):
        lo = j * DIL_SUBQ - HALF_WIN
        hi = lo + nkeys
        parts = []
        if lo < 0:
            parts.append(p_ref[0, :, hs])
        parts.append(c_ref[0, max(lo, 0):min(hi, bq), hs])
        if hi > bq:
            parts.append(n_ref[0, :, hs])
        return jnp.concatenate(parts, axis=0) if len(parts) > 1 else parts[0]

    for j in range(n_sub):
        mask = band
        if j == 0:
            mask = mask & ((c >= HALF_WIN) | (i > 0))
        if j == n_sub - 1:
            mask = mask & ((c < DIL_SUBQ + HALF_WIN) | (i < nblk - 1))
        bias = jnp.where(mask, 0.0, NEG_BIG)
        rows = slice(j * DIL_SUBQ, (j + 1) * DIL_SUBQ)
        lse_all = jnp.zeros((DIL_SUBQ, LANES), F32)
        for h in range(B_HEADS):
            hs = slice(h * B_HDIM, (h + 1) * B_HDIM)
            k = window(kp_ref, kc_ref, kn_ref, j, hs)
            v = window(vp_ref, vc_ref, vn_ref, j, hs)
            sc = _nt_dot(q_ref[0, rows, hs], k) + bias
            m = jnp.max(sc, axis=1, keepdims=True)
            p = jnp.exp2(sc - m).astype(BF16)
            both = jnp.dot(p, jnp.concatenate([v, jnp.ones_like(v)], axis=1), preferred_element_type=F32)
            den = both[:, B_HDIM:]
            o_ref[0, rows, hs] = (both[:, :B_HDIM] * (1.0 / den)).astype(o_ref.dtype)
            lse_all = jnp.where(lane == h, m + jnp.log2(den), lse_all)
        lse_ref[0, rows, :] = lse_all


def _dilated(qv, kv, vv, b, dil, bq):
    m_len = qv.shape[0] // b
    nblk = m_len // bq
    sub = bq // HALF_WIN
    n_half = m_len // HALF_WIN
    view = lambda t: t.reshape(b, m_len, dil * B_WIDTH)
    cur = pl.BlockSpec((1, bq, B_WIDTH), lambda bi, r, i: (bi, i, r))
    prev = pl.BlockSpec((1, HALF_WIN, B_WIDTH), lambda bi, r, i: (bi, jnp.maximum(i * sub - 1, 0), r))
    nxt = pl.BlockSpec((1, HALF_WIN, B_WIDTH), lambda bi, r, i: (bi, jnp.minimum((i + 1) * sub, n_half - 1), r))
    kern = functools.partial(_dilated_kernel, bq=bq, nblk=nblk)
    o, lse = pl.pallas_call(
        kern,
        grid=(b, dil, nblk),
        in_specs=[cur, prev, cur, nxt, prev, cur, nxt],
        out_specs=[cur, pl.BlockSpec((1, bq, LANES), lambda bi, r, i: (bi, i, r))],
        out_shape=[jax.ShapeDtypeStruct((b, m_len, dil * B_WIDTH), MIX_DTYPE),
                   jax.ShapeDtypeStruct((b, m_len, dil * LANES), F32)],
        compiler_params=_cparams(("parallel", "parallel", "parallel")),
        name=f"dilated_d{dil}",
    )(view(qv), view(kv), view(kv), view(kv), view(vv), view(vv), view(vv))
    return o.reshape(b * m_len, dil * B_WIDTH), lse.reshape(b * m_len, dil * LANES)


OUT_GROUPS = 4


def _out_proj_kernel(oa_ref, o1_ref, o2_ref, o3_ref, l1_ref, l2_ref, l3_ref, ga_ref, gb_ref, x_ref,
                     w_ref, goa_ref, gob_ref, lng_ref, lnb_ref, y_ref, on_ref, ln_ref):
    tm = oa_ref.shape[0]
    for i, (o_ref, l_ref, (_, dil)) in enumerate(zip((o1_ref, o2_ref, o3_ref), (l1_ref, l2_ref, l3_ref),
                                                     DILATED_PATTERNS)):
        for r in range(dil):
            rows = pl.ds(r, tm // dil, stride=dil) if dil > 1 else slice(None)
            ln_ref[i, rows, :] = l_ref[:, r * LANES:(r + 1) * LANES]
            for h in range(B_HEADS):
                col = r * B_WIDTH + h * B_HDIM
                on_ref[i, h, rows, :] = o_ref[:, col:col + B_HDIM].astype(F32)
    for g in range(OUT_GROUPS):
        rows = slice(g * tm // OUT_GROUPS, (g + 1) * tm // OUT_GROUPS)
        za = (_rms(oa_ref[rows, :].astype(F32), goa_ref[...]) * ga_ref[rows, :].astype(F32)).astype(BF16)
        l1, l2, l3 = ln_ref[0, rows, :], ln_ref[1, rows, :], ln_ref[2, rows, :]
        mx = jnp.maximum(jnp.maximum(l1, l2), l3)
        e1, e2, e3 = jnp.exp2(l1 - mx), jnp.exp2(l2 - mx), jnp.exp2(l3 - mx)
        inv = 1.0 / (e1 + e2 + e3)
        w1, w2, w3 = e1 * inv, e2 * inv, e3 * inv
        parts = []
        for h in range(B_HEADS):
            parts.append(w1[:, h:h + 1] * on_ref[0, h, rows, :] + w2[:, h:h + 1] * on_ref[1, h, rows, :]
                         + w3[:, h:h + 1] * on_ref[2, h, rows, :])
        ob = jnp.concatenate(parts, axis=1)
        zb = (_rms(ob, gob_ref[...]) * gb_ref[rows, :].astype(F32)).astype(BF16)
        out = (jnp.dot(za, w_ref[:A_WIDTH, :], preferred_element_type=F32)
               + jnp.dot(zb, w_ref[A_WIDTH:, :], preferred_element_type=F32))
        r = DEEPNORM_ALPHA * x_ref[rows, :] + out
        mu = jnp.mean(r, axis=-1, keepdims=True)
        d = r - mu
        var = jnp.mean(d * d, axis=-1, keepdims=True)
        y_ref[rows, :] = d * lax.rsqrt(var + LN_EPS) * lng_ref[...] + lnb_ref[...]


def _out_proj(oa, obs, lses, ga, gb, x2, w_out, goa, gob, lng, lnb, tm):
    n = x2.shape[0]
    row = lambda w: pl.BlockSpec((tm, w), lambda i: (i, 0))
    view = lambda w: [pl.BlockSpec((tm // d, d * w), lambda i: (i, 0)) for _, d in DILATED_PATTERNS]
    full = lambda *shape: pl.BlockSpec(shape, lambda i: (0,) * len(shape))
    n_pat = len(DILATED_PATTERNS)
    return pl.pallas_call(
        _out_proj_kernel,
        grid=(n // tm,),
        in_specs=[row(A_WIDTH)] + view(B_WIDTH) + view(LANES) + [row(A_WIDTH), row(B_WIDTH), row(D_MODEL),
                  pl.BlockSpec((A_WIDTH + B_WIDTH, D_MODEL), lambda i: (0, 0), pipeline_mode=pl.Buffered(1)),
                  full(1, A_WIDTH), full(1, B_WIDTH),
                  full(1, D_MODEL), full(1, D_MODEL)],
        out_specs=row(D_MODEL),
        out_shape=jax.ShapeDtypeStruct((n, D_MODEL), F32),
        scratch_shapes=[pltpu.VMEM((n_pat, B_HEADS, tm, B_HDIM), F32), pltpu.VMEM((n_pat, tm, LANES), F32)],
        compiler_params=_cparams(("parallel",)),
        name="out_proj",
    )(oa, *obs, *lses, ga, gb, x2, w_out, goa, gob, lng, lnb)


def _rope_tables(s):
    pos = jnp.arange(s, dtype=F32)

    def cos_sin(d):
        half = d // 2
        inv = ROPE_THETA ** (-jnp.arange(half, dtype=F32) * 2.0 / d)
        ang = pos[:, None] * inv[None, :]
        return jnp.cos(ang), jnp.sin(ang)

    cb, sb = cos_sin(B_HDIM)
    cos_b = jnp.concatenate([cb, cb], axis=1)
    sin_b = jnp.concatenate([-sb, sb], axis=1)
    ca, sa = cos_sin(A_ROPE)
    zeros = jnp.zeros((s, LANES - A_ROPE), F32)
    cos_k = jnp.concatenate([ca, ca, zeros], axis=1)
    sin_k = jnp.concatenate([-sa, sa, zeros], axis=1)
    return cos_b, sin_b, cos_k, sin_k, ca.T, sa.T


def _prep_weights(w_in, g_qn, w_uq, g_kvn, w_ukv, g_oa, g_ob, w_out, ln_g, ln_b):
    n_lat = Q_LORA + KV_LORA + A_ROPE
    w_lat = jnp.pad(w_in[:, :n_lat], ((0, 0), (0, LAT_PAD))).astype(BF16)
    w_rest = w_in[:, n_lat:].astype(BF16)
    wuqt = w_uq.reshape(Q_LORA, A_HEADS, A_QK).transpose(1, 2, 0).astype(BF16)
    wukv = w_ukv.reshape(KV_LORA, A_HEADS, A_NOPE + A_VDIM)
    wuk = wukv[:, :, :A_NOPE].transpose(1, 0, 2).astype(BF16)
    wuvt = wukv[:, :, A_NOPE:].transpose(1, 2, 0).astype(BF16)
    return dict(w_lat=w_lat, w_rest=w_rest, gq=g_qn[None, :], gkv=g_kvn[None, :], wuqt=wuqt, wuk=wuk, wuvt=wuvt,
                goa=g_oa[None, :], gob=g_ob[None, :], w_out=w_out.astype(BF16),
                lng=ln_g[None, :], lnb=ln_b[None, :])


def _tiles(s):
    pick = lambda pref, n: pref if n % pref == 0 else n
    bq = pick(min(FLASH_BQ_MAX, max(s // 4, LANES)), s)
    return dict(tm_in=pick(256, s), tm_up=pick(1024, s), bq=bq, bk=pick(512, s), tm_out=pick(512, s))


def _encoder_layer(x, p):
    b, s, _ = x.shape
    longest = max(w // 2 for w, _ in DILATED_PATTERNS)
    assert s % longest == 0, "sequence length must be a multiple of the largest dilated chunk"
    t = _tiles(s)
    cos_b, sin_b, cos_k, sin_k, cos_t, sin_t = _rope_tables(s)
    x2 = x.reshape(b * s, D_MODEL)
    lat, qvs, kvs, vvs, ga, gb = _in_proj(x2, p["w_lat"], p["w_rest"], cos_b, sin_b, s, t["tm_in"])
    qt, k, vt = _mla_up(lat.reshape(b, s, SEG), p["gq"], p["gkv"], p["wuqt"], p["wuk"], p["wuvt"],
                        cos_k, sin_k, cos_t, sin_t, t["tm_up"])
    oa = _mla_flash(qt, k, vt, t["bq"], t["bk"]).reshape(b * s, A_WIDTH)
    obs, lses = [], []
    for i, (window, dil) in enumerate(DILATED_PATTERNS):
        assert window // (2 * dil) == HALF_WIN
        o, lse = _dilated(qvs[i], kvs[i], vvs[i], b, dil, min(DIL_BQ, s // dil))
        obs.append(o)
        lses.append(lse)
    y = _out_proj(oa, obs, lses, ga, gb, x2, p["w_out"], p["goa"], p["gob"], p["lng"], p["lnb"], t["tm_out"])
    return y.reshape(b, s, D_MODEL)


def kernel(x_prompt, x_sample, w_in, g_qn, w_uq, g_kvn, w_ukv, g_oa, g_ob, w_out, ln_g, ln_b):
    layers = [_prep_weights(w_in[l], g_qn[l], w_uq[l], g_kvn[l], w_ukv[l], g_oa[l], g_ob[l],
                            w_out[l], ln_g[l], ln_b[l]) for l in range(DEPTH)]

    def trunk(x):
        for p in layers:
            x = _encoder_layer(x, p)
        return x

    return (trunk(x_prompt), trunk(x_sample))
```
